```python
import math
import jax, jax.numpy as jnp
from jax import lax
import numpy as np

D_MODEL = 1024
BATCH = 8
SEQ = 2048
DEPTH = 2
DEC_BATCH = 128
DEC_SEQ = 8
PAST_LEN = 16384
PAGE_SIZE = 128

D_MIX = D_MODEL
HEAD_DIM = 64
LRU_WIDTH = D_MIX // 4
LRU_BLOCKS = 4
LRU_BLOCK = LRU_WIDTH // LRU_BLOCKS
LRU_C = 8.0
RET_WIDTH = (D_MIX - LRU_WIDTH) // 2
GDN_WIDTH = D_MIX - LRU_WIDTH - RET_WIDTH
RET_HEADS = RET_WIDTH // HEAD_DIM
GDN_HEADS = GDN_WIDTH // HEAD_DIM
CONV_W = 4
CHUNK = 64
ROPE_BASE = 10000.0
PE_DIM = 256
N_GROUPS = 4
EXPERTS_PER_GROUP = 4
N_EXPERTS = N_GROUPS * EXPERTS_PER_GROUP
TOP_K = 2
D_EXPERT = D_MODEL // 4
EPS = 1e-6
IN_SIZES = (LRU_WIDTH, LRU_WIDTH, RET_WIDTH, RET_WIDTH, RET_WIDTH, RET_WIDTH,
            GDN_WIDTH, GDN_WIDTH, GDN_WIDTH, GDN_WIDTH, GDN_HEADS, GDN_HEADS)
D_IN = sum(IN_SIZES)

kernel_name = "hymba_lru_retnet_gdn_hiermoe_step"

F32 = jnp.float32


def rmsnorm(x, g):
    xf = x.astype(F32)
    y = xf * lax.rsqrt(jnp.mean(xf * xf, axis=-1, keepdims=True) + EPS)
    return (y * g.astype(F32)).astype(x.dtype)


def head_rms(x):
    return x * lax.rsqrt(jnp.mean(x * x, axis=-1, keepdims=True) + EPS)


def l2norm(x):
    return x * lax.rsqrt(jnp.sum(x * x, axis=-1, keepdims=True) + EPS)


def causal_conv(x, buf, w):
    T = x.shape[1]
    xp = jnp.concatenate([buf.astype(x.dtype), x], axis=1)
    y = sum(xp[:, j:j + T] * w[j] for j in range(CONV_W))
    return y, xp[:, -(CONV_W - 1):]


def rope(x, pos):
    half = HEAD_DIM // 2
    inv = jnp.exp(-math.log(ROPE_BASE) * jnp.arange(half, dtype=F32) / half)
    ang = pos.astype(F32)[:, None] * inv[None]
    cos = jnp.cos(ang)[None, :, None]
    sin = jnp.sin(ang)[None, :, None]
    x1, x2 = x[..., :half], x[..., half:]
    return jnp.concatenate([x1 * cos - x2 * sin, x1 * sin + x2 * cos], axis=-1)


def rg_lru(x, h0, wa, ba, wx, bx, lam):
    B, T, _ = x.shape
    xb = x.reshape(B, T, LRU_BLOCKS, LRU_BLOCK)
    r = jax.nn.sigmoid(jnp.einsum('btnc,ncd->btnd', xb, wa).reshape(B, T, LRU_WIDTH) + ba).astype(F32)
    i = jax.nn.sigmoid(jnp.einsum('btnc,ncd->btnd', xb, wx).reshape(B, T, LRU_WIDTH) + bx).astype(F32)
    log_a = -LRU_C * r * jax.nn.softplus(-lam.astype(F32))
    a = jnp.exp(log_a)
    b = jnp.sqrt(-jnp.expm1(2.0 * log_a)) * (i * x.astype(F32))

    def comb(l, rr):
        return (l[0] * rr[0], rr[0] * l[1] + rr[1])

    a_cum, b_cum = lax.associative_scan(comb, (a, b), axis=1)
    h = a_cum * h0.astype(F32)[:, None] + b_cum
    return h, h[:, -1]


def retention(q, k, v, s0, C):
    B, T, H, dk = q.shape
    dv = v.shape[-1]
    N = T // C
    lg = jnp.log(1.0 - 2.0 ** (-5.0 - jnp.arange(H, dtype=F32)))
    q = q.reshape(B, N, C, H, dk)
    k = k.reshape(B, N, C, H, dk)
    v = v.reshape(B, N, C, H, dv)
    n = jnp.arange(C, dtype=F32)
    rel = n[:, None] - n[None, :]
    dmat = jnp.where(rel[None] >= 0, jnp.exp(jnp.maximum(rel, 0.0)[None] * lg[:, None, None]), 0.0)
    scores = jnp.einsum('bnihd,bnjhd->bnhij', q, k) * dmat
    o_inner = jnp.einsum('bnhij,bnjhe->bnihe', scores, v)
    k_dec = k * jnp.exp((C - 1.0 - n)[:, None] * lg[None, :])[:, :, None]
    kv = jnp.einsum('bnjhd,bnjhe->nbhde', k_dec, v)
    g_c = jnp.exp(C * lg)[:, None, None]

    def step(s, kv_c):
        return g_c * s + kv_c, s

    s_fin, s_starts = lax.scan(step, s0, kv)
    q_dec = q * jnp.exp((n + 1.0)[:, None] * lg[None, :])[:, :, None]
    o_cross = jnp.einsum('bnihd,nbhde->bnihe', q_dec, s_starts)
    return (o_inner + o_cross).reshape(B, T, H, dv), s_fin


def gated_delta(q, k, v, g, beta, s0, C):
    B, T, H, dk = q.shape
    dv = v.shape[-1]
    N = T // C
    ch = lambda t, d: t.reshape(B, N, C, H, d).transpose(0, 1, 3, 2, 4)
    q, k, v = ch(q, dk), ch(k, dk), ch(v, dv)
    g = g.reshape(B, N, C, H).transpose(0, 1, 3, 2)
    beta = beta.reshape(B, N, C, H).transpose(0, 1, 3, 2)
    G = jnp.cumsum(g, axis=-1)
    idx = jnp.arange(C)
    lower = idx[:, None] >= idx[None, :]
    strict = idx[:, None] > idx[None, :]
    decay = jnp.exp(jnp.where(lower, G[..., :, None] - G[..., None, :], -jnp.inf))
    kk = jnp.einsum('bnhid,bnhjd->bnhij', k, k)
    A = jnp.where(strict, beta[..., :, None] * kk * decay, 0.0)
    eye = jnp.eye(C, dtype=F32)
    tm = lax.linalg.triangular_solve(eye + A, jnp.broadcast_to(eye, A.shape), left_side=True,
                                     lower=True, unit_diagonal=True)
    u = tm @ (beta[..., None] * v)
    w = tm @ (beta[..., None] * jnp.exp(G)[..., None] * k)
    P = jnp.einsum('bnhid,bnhjd->bnhij', q, k) * decay
    qg = q * jnp.exp(G)[..., None]
    kd = k * jnp.exp(G[..., -1:] - G)[..., None]
    gl = jnp.exp(G[..., -1])
    xs = (jnp.swapaxes(u, 0, 1), jnp.swapaxes(w, 0, 1), jnp.swapaxes(qg, 0, 1),
          jnp.swapaxes(P, 0, 1), jnp.swapaxes(kd, 0, 1), jnp.swapaxes(gl, 0, 1))

    def step(s, xc):
        uc, wc, qc, pc, kc, gc = xc
        vn = uc - wc @ s
        o = qc @ s + pc @ vn
        s = gc[..., None, None] * s + jnp.swapaxes(kc, -1, -2) @ vn
        return s, o

    s_fin, o = lax.scan(step, s0, xs)
    return o.transpose(1, 0, 3, 2, 4).reshape(B, T, H, dv), s_fin


def mixer(xn, pos0, st, W, l):
    lru_conv, lru_h, ret_s, gdn_conv, gdn_s = st
    B, T, _ = xn.shape
    dt = xn.dtype
    C = math.gcd(T, CHUNK)
    proj = xn @ W['w_in'][l]
    splits = np.cumsum(IN_SIZES)[:-1].tolist()
    lx, lgate, rq, rk, rv, rg, gq, gk, gv, gz, ga, gb = jnp.split(proj, splits, axis=-1)
    lx_c, new_lru_conv = causal_conv(lx, lru_conv, W['lru_conv_w'][l])
    lx_c = lx_c + W['lru_conv_b'][l]
    h, new_lru_h = rg_lru(lx_c, lru_h, W['lru_wa'][l], W['lru_ba'][l], W['lru_wx'][l], W['lru_bx'][l],
                          W['lru_lambda'][l])
    out_a = h * jax.nn.gelu(lgate.astype(F32))
    pos = pos0 + jnp.arange(T)
    hs = lambda t, H: t.reshape(B, T, H, HEAD_DIM).astype(F32)
    q_r = rope(hs(rq, RET_HEADS), pos) * HEAD_DIM ** -0.5
    k_r = rope(hs(rk, RET_HEADS), pos)
    o_r, new_ret = retention(q_r, k_r, hs(rv, RET_HEADS), ret_s.astype(F32), C)
    out_b = head_rms(o_r).reshape(B, T, RET_WIDTH) * W['ret_norm_g'][l].astype(F32) * jax.nn.silu(rg.astype(F32))
    qkv = jnp.concatenate([gq, gk, gv], axis=-1)
    qkv_c, new_gdn_conv = causal_conv(qkv, gdn_conv, W['gdn_conv_w'][l])
    qkv_c = jax.nn.silu(qkv_c)
    cq, ck, cv = jnp.split(qkv_c, [GDN_WIDTH, 2 * GDN_WIDTH], axis=-1)
    q_g = l2norm(hs(cq, GDN_HEADS)) * HEAD_DIM ** -0.5
    k_g = l2norm(hs(ck, GDN_HEADS))
    g_log = -jnp.exp(W['gdn_a_log'][l].astype(F32)) * jax.nn.softplus(ga.astype(F32) + W['gdn_dt_bias'][l])
    beta = jax.nn.sigmoid(gb.astype(F32))
    o_g, new_gdn = gated_delta(q_g, k_g, hs(cv, GDN_HEADS), g_log, beta, gdn_s.astype(F32), C)
    out_c = (head_rms(o_g) * W['gdn_norm_g'][l].astype(F32)).reshape(B, T, GDN_WIDTH) * jax.nn.silu(gz.astype(F32))
    merged = jnp.concatenate([out_a, out_b, out_c], axis=-1).astype(dt)
    out = merged @ W['w_out'][l]
    return out, (new_lru_conv, new_lru_h, new_ret, new_gdn_conv, new_gdn)


def hier_moe(xn, W, l):
    B, T, D = xn.shape
    x2 = xn.reshape(-1, D)
    gl = (x2 @ W['w_router_group'][l]).astype(F32) + W['b_router_group'][l]
    gp = jax.nn.softmax(gl, axis=-1)
    gsel = jnp.argmax(gl, axis=-1)
    g_w = jnp.take_along_axis(gp, gsel[:, None], axis=-1)
    el = ((x2 @ W['w_router_expert'][l]).astype(F32) + W['b_router_expert'][l]).reshape(-1, N_GROUPS, EXPERTS_PER_GROUP)
    el_sel = jnp.take_along_axis(el, gsel[:, None, None], axis=1)[:, 0]
    ep = jax.nn.softmax(el_sel, axis=-1)
    top_w, top_i = lax.top_k(ep, TOP_K)
    top_w = top_w / jnp.sum(top_w, axis=-1, keepdims=True) * g_w
    eidx = gsel[:, None] * EXPERTS_PER_GROUP + top_i
    gates = jnp.sum(jax.nn.one_hot(eidx, N_EXPERTS, dtype=F32) * top_w[..., None], axis=1)
    hg = jnp.einsum('nd,edf->nef', x2, W['w_expert_gate'][l])
    hu = jnp.einsum('nd,edf->nef', x2, W['w_expert_up'][l])
    hmid = (jax.nn.silu(hg) * hu * gates[:, :, None]).astype(xn.dtype)
    y = jnp.einsum('nef,efd->nd', hmid, W['w_expert_down'][l])
    return y.reshape(B, T, D)


def trunk(x, p, states, pos0, W, final_norm_g):
    h = x
    new = ([], [], [], [], [])
    for l in range(DEPTH):
        st = tuple(s[l] for s in states)
        mix, nst = mixer(rmsnorm(h, W['norm_mix_g'][l]), pos0, st, W, l)
        h = h + mix
        h = h + hier_moe(rmsnorm(h, W['norm_ffn_g'][l]), W, l)
        gate = jax.nn.sigmoid(rmsnorm(h, W['norm_pe_g'][l]) @ W['w_pe_gate'][l])
        h = h + (p[l] @ W['w_pe'][l]) * gate
        for lst, s in zip(new, nst):
            lst.append(s.astype(x.dtype))
    y = rmsnorm(h, final_norm_g)
    return y, tuple(jnp.stack(lst) for lst in new)


def setup_inputs(seed: int = 0) -> dict:
    key = jax.random.key(seed)
    ks = iter(jax.random.split(key, 48))
    nrm = lambda shape, scale: jax.random.normal(next(ks), shape, F32) * scale
    gain = lambda shape: 1.0 + jax.random.normal(next(ks), shape, F32) * 0.01
    L = DEPTH
    u = jax.random.uniform(next(ks), (L, LRU_WIDTH), F32, 0.9, 0.999)
    a0 = u ** (1.0 / LRU_C)
    lru_lambda = jnp.log(a0) - jnp.log1p(-a0)
    dt0 = jnp.exp(jax.random.uniform(next(ks), (L, GDN_HEADS), F32, math.log(1e-3), math.log(1e-1)))
    gdn_dt_bias = dt0 + jnp.log(-jnp.expm1(-dt0))
    gdn_a_log = jnp.log(jax.random.uniform(next(ks), (L, GDN_HEADS), F32, 1.0, 16.0))
    return {
        'x_prompt': nrm((BATCH, SEQ, D_MODEL), 1.0),
        'x_sample': nrm((DEC_BATCH, DEC_SEQ, D_MODEL), 1.0),
        'p_prompt': nrm((L, BATCH, SEQ, PE_DIM), 1.0),
        'p_sample': nrm((L, DEC_BATCH, DEC_SEQ, PE_DIM), 1.0),
        'state_lru_conv': nrm((L, DEC_BATCH, CONV_W - 1, LRU_WIDTH), 1.0),
        'state_lru_h': nrm((L, DEC_BATCH, LRU_WIDTH), 0.5),
        'state_ret': nrm((L, DEC_BATCH, RET_HEADS, HEAD_DIM, HEAD_DIM), 0.1),
        'state_gdn_conv': nrm((L, DEC_BATCH, CONV_W - 1, 3 * GDN_WIDTH), 1.0),
        'state_gdn': nrm((L, DEC_BATCH, GDN_HEADS, HEAD_DIM, HEAD_DIM), 0.1),
        'norm_mix_g': gain((L, D_MODEL)),
        'w_in': nrm((L, D_MODEL, D_IN), D_MODEL ** -0.5),
        'lru_conv_w': nrm((L, CONV_W, LRU_WIDTH), CONV_W ** -0.5),
        'lru_conv_b': nrm((L, LRU_WIDTH), 0.01),
        'lru_wa': nrm((L, LRU_BLOCKS, LRU_BLOCK, LRU_BLOCK), LRU_BLOCK ** -0.5),
        'lru_ba': nrm((L, LRU_WIDTH), 0.01),
        'lru_wx': nrm((L, LRU_BLOCKS, LRU_BLOCK, LRU_BLOCK), LRU_BLOCK ** -0.5),
        'lru_bx': nrm((L, LRU_WIDTH), 0.01),
        'lru_lambda': lru_lambda,
        'ret_norm_g': gain((L, RET_WIDTH)),
        'gdn_conv_w': nrm((L, CONV_W, 3 * GDN_WIDTH), CONV_W ** -0.5),
        'gdn_a_log': gdn_a_log,
        'gdn_dt_bias': gdn_dt_bias,
        'gdn_norm_g': gain((L, HEAD_DIM)),
        'w_out': nrm((L, D_MIX, D_MODEL), D_MIX ** -0.5),
        'norm_ffn_g': gain((L, D_MODEL)),
        'w_router_group': nrm((L, D_MODEL, N_GROUPS), D_MODEL ** -0.5),
        'b_router_group': nrm((L, N_GROUPS), 0.01),
        'w_router_expert': nrm((L, D_MODEL, N_EXPERTS), D_MODEL ** -0.5),
        'b_router_expert': nrm((L, N_EXPERTS), 0.01),
        'w_expert_gate': nrm((L, N_EXPERTS, D_MODEL, D_EXPERT), D_MODEL ** -0.5),
        'w_expert_up': nrm((L, N_EXPERTS, D_MODEL, D_EXPERT), D_MODEL ** -0.5),
        'w_expert_down': nrm((L, N_EXPERTS, D_EXPERT, D_MODEL), D_EXPERT ** -0.5),
        'norm_pe_g': gain((L, D_MODEL)),
        'w_pe': nrm((L, PE_DIM, D_MODEL), PE_DIM ** -0.5),
        'w_pe_gate': nrm((L, D_MODEL, D_MODEL), D_MODEL ** -0.5),
        'final_norm_g': gain((D_MODEL,)),
    }


def reference(x_prompt, x_sample, p_prompt, p_sample, state_lru_conv, state_lru_h, state_ret,
              state_gdn_conv, state_gdn, norm_mix_g, w_in, lru_conv_w, lru_conv_b, lru_wa, lru_ba,
              lru_wx, lru_bx, lru_lambda, ret_norm_g, gdn_conv_w, gdn_a_log, gdn_dt_bias, gdn_norm_g,
              w_out, norm_ffn_g, w_router_group, b_router_group, w_router_expert, b_router_expert,
              w_expert_gate, w_expert_up, w_expert_down, norm_pe_g, w_pe, w_pe_gate, final_norm_g):
    W = dict(norm_mix_g=norm_mix_g, w_in=w_in, lru_conv_w=lru_conv_w, lru_conv_b=lru_conv_b,
             lru_wa=lru_wa, lru_ba=lru_ba, lru_wx=lru_wx, lru_bx=lru_bx, lru_lambda=lru_lambda,
             ret_norm_g=ret_norm_g, gdn_conv_w=gdn_conv_w, gdn_a_log=gdn_a_log, gdn_dt_bias=gdn_dt_bias,
             gdn_norm_g=gdn_norm_g, w_out=w_out, norm_ffn_g=norm_ffn_g, w_router_group=w_router_group,
             b_router_group=b_router_group, w_router_expert=w_router_expert, b_router_expert=b_router_expert,
             w_expert_gate=w_expert_gate, w_expert_up=w_expert_up, w_expert_down=w_expert_down,
             norm_pe_g=norm_pe_g, w_pe=w_pe, w_pe_gate=w_pe_gate)
    bp = x_prompt.shape[0]
    dt = x_prompt.dtype
    zero_states = (jnp.zeros((DEPTH, bp, CONV_W - 1, LRU_WIDTH), dt),
                   jnp.zeros((DEPTH, bp, LRU_WIDTH), dt),
                   jnp.zeros((DEPTH, bp, RET_HEADS, HEAD_DIM, HEAD_DIM), dt),
                   jnp.zeros((DEPTH, bp, CONV_W - 1, 3 * GDN_WIDTH), dt),
                   jnp.zeros((DEPTH, bp, GDN_HEADS, HEAD_DIM, HEAD_DIM), dt))
    y_prompt, st_p = trunk(x_prompt, p_prompt, zero_states, 0, W, final_norm_g)
    sample_states = (state_lru_conv, state_lru_h, state_ret, state_gdn_conv, state_gdn)
    y_sample, st_s = trunk(x_sample, p_sample, sample_states, PAST_LEN, W, final_norm_g)
    new_lru_conv_prompt, new_lru_h_prompt, new_ret_prompt, new_gdn_conv_prompt, new_gdn_prompt = st_p
    new_lru_conv_sample, new_lru_h_sample, new_ret_sample, new_gdn_conv_sample, new_gdn_sample = st_s
    return (y_prompt, y_sample, new_lru_conv_prompt, new_lru_h_prompt, new_ret_prompt, new_gdn_conv_prompt,
            new_gdn_prompt, new_lru_conv_sample, new_lru_h_sample, new_ret_sample, new_gdn_conv_sample,
            new_gdn_sample)
```

```python
import functools
import math

import jax
import jax.numpy as jnp
import numpy as np
from jax import lax
from jax.experimental import pallas as pl
from jax.experimental.pallas import tpu as pltpu

F32 = jnp.float32
BF16 = jnp.bfloat16

D_MODEL = 1024
DEPTH = 2
PAST_LEN = 16384
HEAD_DIM = 64
HALF = HEAD_DIM // 2
LRU_WIDTH = 256
LRU_BLOCKS = 4
LRU_BLOCK = 64
LRU_C = 8.0
RET_WIDTH = 384
GDN_WIDTH = 384
N_HEADS = 6
CONV_W = 4
CHUNK = 64
ROPE_BASE = 10000.0
PE_DIM = 256
N_GROUPS = 4
EXPERTS_PER_GROUP = 4
N_EXPERTS = 16
D_EXPERT = 256
EPS = 1e-6

LANES = 128
SUBLANES = 8
VMEM_LIMIT = 48 * 1024 * 1024

LRU_COLS = 2 * LRU_WIDTH
RET_COLS = 4 * RET_WIDTH
GDN_COLS = 4 * GDN_WIDTH + LANES
IN_COLS = LRU_COLS + RET_COLS + GDN_COLS
ROUTER_GROUP_LANE = 0
ROUTER_EXPERT_LANE = N_GROUPS
GDN_BETA_LANE = N_HEADS

TOKEN_TILE = 256
MOE_TILE = 1024
PROMPT_BLOCK = 256
SAMPLE_SEQS = 16


def _cparams(sem):
    return pltpu.CompilerParams(dimension_semantics=sem, vmem_limit_bytes=VMEM_LIMIT)


def _rmsnorm(x, g):
    return x * lax.rsqrt(jnp.mean(x * x, axis=-1, keepdims=True) + EPS) * g


def _dot(a, b, **kw):
    return jnp.dot(a, b, preferred_element_type=F32, **kw)


def _dot_nt(a, b):
    return lax.dot_general(a, b, (((1,), (1,)), ((), ())), preferred_element_type=F32)


def _dot_tn(a, b, **kw):
    return lax.dot_general(a, b, (((0,), (0,)), ((), ())), preferred_element_type=F32, **kw)


def _softplus(x):
    return jnp.maximum(x, 0.0) + jnp.log1p(jnp.exp(-jnp.abs(x)))


def _silu(x):
    return x * jax.nn.sigmoid(x)


def _gelu_tanh(x):
    return 0.5 * x * (1.0 + jnp.tanh(math.sqrt(2.0 / math.pi) * (x + 0.044715 * (x * x * x))))


def _in_proj_kernel(h_ref, g_ref, w_ref, lru_ref, ret_ref, gdn_ref):
    xn = _rmsnorm(h_ref[...], g_ref[...]).astype(BF16)
    lru_ref[...] = _dot(xn, w_ref[:, 0:LRU_COLS])
    ret_ref[...] = _dot(xn, w_ref[:, LRU_COLS:LRU_COLS + RET_COLS])
    gdn_ref[...] = _dot(xn, w_ref[:, LRU_COLS + RET_COLS:IN_COLS])


def _in_proj(h, g, w):
    n = h.shape[0]
    tm = TOKEN_TILE
    row = lambda i: (i, 0)
    const = lambda i: (0, 0)
    return pl.pallas_call(
        _in_proj_kernel,
        grid=(n // tm,),
        in_specs=[pl.BlockSpec((tm, D_MODEL), row),
                  pl.BlockSpec((1, D_MODEL), const),
                  pl.BlockSpec((D_MODEL, IN_COLS), const)],
        out_specs=[pl.BlockSpec((tm, LRU_COLS), row),
                   pl.BlockSpec((tm, RET_COLS), row),
                   pl.BlockSpec((tm, GDN_COLS), row)],
        out_shape=[jax.ShapeDtypeStruct((n, LRU_COLS), F32),
                   jax.ShapeDtypeStruct((n, RET_COLS), F32),
                   jax.ShapeDtypeStruct((n, GDN_COLS), F32)],
        compiler_params=_cparams(("parallel",)),
        name="in_proj",
    )(h, g, w)


def _route(logits):
    lane = lax.broadcasted_iota(jnp.int32, logits.shape, 1).astype(F32)
    neg = -jnp.inf
    far = float(LANES)
    gmask = lane < N_GROUPS
    gl = jnp.where(gmask, logits, neg)
    gmax = jnp.max(gl, axis=-1, keepdims=True)
    gsel = jnp.min(jnp.where(gl == gmax, lane, far), axis=-1, keepdims=True)
    gsum = jnp.sum(jnp.where(gmask, jnp.exp(logits - gmax), 0.0), axis=-1, keepdims=True)
    g_w = 1.0 / gsum
    lo = ROUTER_EXPERT_LANE + EXPERTS_PER_GROUP * gsel
    el = jnp.where(lane >= lo, jnp.where(lane < lo + EXPERTS_PER_GROUP, logits, neg), neg)
    m1 = jnp.max(el, axis=-1, keepdims=True)
    i1 = jnp.min(jnp.where(el == m1, lane, far), axis=-1, keepdims=True)
    el2 = jnp.where(lane == i1, neg, el)
    m2 = jnp.max(el2, axis=-1, keepdims=True)
    i2 = jnp.min(jnp.where(el2 == m2, lane, far), axis=-1, keepdims=True)
    e2 = jnp.exp(m2 - m1)
    w1 = g_w / (1.0 + e2)
    w2 = w1 * e2
    return jnp.where(lane == i1, w1, 0.0) + jnp.where(lane == i2, w2, 0.0)


def _out_router_kernel(h_ref, a_ref, b_ref, c_ref, wo_ref, g_ref, wr_ref, br_ref,
                       h1_ref, xn_ref, gates_ref):
    o1 = LRU_WIDTH
    o2 = LRU_WIDTH + RET_WIDTH
    mix = (_dot(a_ref[...].astype(BF16), wo_ref[0:o1, :])
           + _dot(b_ref[...].astype(BF16), wo_ref[o1:o2, :])
           + _dot(c_ref[...].astype(BF16), wo_ref[o2:D_MODEL, :]))
    h1 = h_ref[...] + mix
    h1_ref[...] = h1
    xn = _rmsnorm(h1, g_ref[...])
    xn_ref[...] = xn.astype(BF16)
    logits = _dot(xn, wr_ref[...], precision=lax.Precision.HIGHEST) + br_ref[...]
    gates_ref[...] = _route(logits)


def _out_router(h, a, b, c, wo, g, wr, br):
    n = h.shape[0]
    tm = TOKEN_TILE
    row = lambda i: (i, 0)
    const = lambda i: (0, 0)
    return pl.pallas_call(
        _out_router_kernel,
        grid=(n // tm,),
        in_specs=[pl.BlockSpec((tm, D_MODEL), row),
                  pl.BlockSpec((tm, LRU_WIDTH), row),
                  pl.BlockSpec((tm, RET_WIDTH), row),
                  pl.BlockSpec((tm, GDN_WIDTH), row),
                  pl.BlockSpec((D_MODEL, D_MODEL), const),
                  pl.BlockSpec((1, D_MODEL), const),
                  pl.BlockSpec((D_MODEL, LANES), const),
                  pl.BlockSpec((1, LANES), const)],
        out_specs=[pl.BlockSpec((tm, D_MODEL), row),
                   pl.BlockSpec((tm, D_MODEL), row),
                   pl.BlockSpec((tm, LANES), row)],
        out_shape=[jax.ShapeDtypeStruct((n, D_MODEL), F32),
                   jax.ShapeDtypeStruct((n, D_MODEL), BF16),
                   jax.ShapeDtypeStruct((n, LANES), F32)],
        compiler_params=_cparams(("parallel",)),
        name="out_router",
    )(h, a, b, c, wo, g, wr, br)


def _moe_kernel(xn_ref, gates_ref, h1_ref, wg_ref, wu_ref, wd_ref, out_ref):
    e = pl.program_id(1)

    @pl.when(e == 0)
    def _():
        out_ref[...] = h1_ref[...]

    xn = xn_ref[...]
    hg = _dot(xn, wg_ref[...])
    hu = _dot(xn, wu_ref[...])
    gates = gates_ref[...]
    lane = lax.broadcasted_iota(jnp.int32, gates.shape, 1)
    gcol = jnp.sum(jnp.where(lane == e + ROUTER_EXPERT_LANE, gates, 0.0), axis=-1, keepdims=True)
    hmid = (_silu(hg) * hu * gcol).astype(BF16)
    out_ref[...] += _dot(hmid, wd_ref[...])


def _moe(xn, gates, h1, wg, wu, wd):
    n = xn.shape[0]
    tm = MOE_TILE if n % MOE_TILE == 0 else TOKEN_TILE
    row = lambda i, e: (i, 0)
    return pl.pallas_call(
        _moe_kernel,
        grid=(n // tm, N_EXPERTS),
        in_specs=[pl.BlockSpec((tm, D_MODEL), row),
                  pl.BlockSpec((tm, LANES), row),
                  pl.BlockSpec((tm, D_MODEL), row),
                  pl.BlockSpec((None, D_MODEL, D_EXPERT), lambda i, e: (e, 0, 0)),
                  pl.BlockSpec((None, D_MODEL, D_EXPERT), lambda i, e: (e, 0, 0)),
                  pl.BlockSpec((None, D_EXPERT, D_MODEL), lambda i, e: (e, 0, 0))],
        out_specs=pl.BlockSpec((tm, D_MODEL), row),
        out_shape=jax.ShapeDtypeStruct((n, D_MODEL), F32),
        compiler_params=_cparams(("parallel", "arbitrary")),
        name="moe",
    )(xn, gates, h1, wg, wu, wd)


def _pe_kernel(h_ref, p_ref, g_ref, wgate_ref, wpe_ref, fg_ref, out_ref, *, final):
    h = h_ref[...]
    xn = _rmsnorm(h, g_ref[...]).astype(BF16)
    gate = jax.nn.sigmoid(_dot(xn, wgate_ref[...]))
    pe = _dot(p_ref[...].astype(BF16), wpe_ref[...])
    h3 = h + pe * gate
    out_ref[...] = _rmsnorm(h3, fg_ref[...]) if final else h3


def _pe(h, p, g, wgate, wpe, fg, final):
    n = h.shape[0]
    tm = TOKEN_TILE
    row = lambda i: (i, 0)
    const = lambda i: (0, 0)
    return pl.pallas_call(
        functools.partial(_pe_kernel, final=final),
        grid=(n // tm,),
        in_specs=[pl.BlockSpec((tm, D_MODEL), row),
                  pl.BlockSpec((tm, PE_DIM), row),
                  pl.BlockSpec((1, D_MODEL), const),
                  pl.BlockSpec((D_MODEL, D_MODEL), const),
                  pl.BlockSpec((PE_DIM, D_MODEL), const),
                  pl.BlockSpec((1, D_MODEL), const)],
        out_specs=pl.BlockSpec((tm, D_MODEL), row),
        out_shape=jax.ShapeDtypeStruct((n, D_MODEL), F32),
        compiler_params=_cparams(("parallel",)),
        name="pe_gate",
    )(h, p, g, wgate, wpe, fg)


def _conv_unit(ext, x, w_ref, c):
    ext[SUBLANES:SUBLANES + c, :] = x
    y = w_ref[CONV_W - 1:CONV_W, :] * x
    for j in range(CONV_W - 1):
        off = SUBLANES - (CONV_W - 1) + j
        y = y + w_ref[j:j + 1, :] * ext[off:off + c, :]
    return y


def _lru_kernel(*refs, c, units, carry):
    if carry:
        (x_ref, cw_ref, cb_ref, wa_ref, ba_ref, wx_ref, bx_ref, lam_ref,
         out_ref, nconv_ref, nh_ref, ext, y_scr, a_scr, b_scr, hcar) = refs
    else:
        (x_ref, cw_ref, cb_ref, wa_ref, ba_ref, wx_ref, bx_ref, lam_ref, buf_ref, h0_ref,
         out_ref, nconv_ref, nh_ref, ext, y_scr, a_scr, b_scr) = refs
    w = LRU_WIDTH
    hist = SUBLANES - (CONV_W - 1)
    if carry:
        t = pl.program_id(1)
        last = pl.num_programs(1) - 1

        @pl.when(t == 0)
        def _():
            ext[0:SUBLANES, :] = jnp.zeros((SUBLANES, w), F32)
            hcar[...] = jnp.zeros((1, w), F32)

    def conv_body(u, carry_):
        r0 = pl.multiple_of(u * c, SUBLANES)
        if not carry:
            ext[hist:SUBLANES, :] = buf_ref[u]
        x = x_ref[pl.ds(r0, c), 0:w]
        y_scr[pl.ds(r0, c), :] = _conv_unit(ext, x, cw_ref, c) + cb_ref[...]
        if carry:
            @pl.when(t == last)
            def _():
                nconv_ref[0] = ext[c + hist:c + SUBLANES, :]
            ext[0:SUBLANES, :] = ext[c:c + SUBLANES, :]
        else:
            nconv_ref[u] = ext[c + hist:c + SUBLANES, :]
        return carry_

    lax.fori_loop(0, units, conv_body, 0)

    y = y_scr[...]
    yb = y.astype(BF16)
    r = jax.nn.sigmoid(_dot(yb, wa_ref[...]) + ba_ref[...])
    i = jax.nn.sigmoid(_dot(yb, wx_ref[...]) + bx_ref[...])
    log_a = (-LRU_C) * r * _softplus(-lam_ref[...])
    th = jnp.tanh(log_a)
    a_scr[...] = jnp.exp(log_a)
    b_scr[...] = jnp.sqrt(-2.0 * th / (1.0 - th)) * (i * y)

    def scan_body(u, carry_):
        r0 = pl.multiple_of(u * c, SUBLANES)
        a = a_scr[pl.ds(r0, c), :]
        b = b_scr[pl.ds(r0, c), :]
        rowi = lax.broadcasted_iota(jnp.int32, (c, w), 0)
        d = 1
        while d < c:
            keep = rowi >= d
            a_s = pltpu.roll(a, d, axis=0)
            b_s = pltpu.roll(b, d, axis=0)
            b = jnp.where(keep, a * b_s + b, b)
            a = jnp.where(keep, a * a_s, a)
            d *= 2
        h0 = hcar[...] if carry else h0_ref[u]
        h = a * h0 + b
        gate = x_ref[pl.ds(r0, c), w:2 * w]
        out_ref[pl.ds(r0, c), :] = h * _gelu_tanh(gate)
        if carry:
            hcar[...] = h[c - 1:c, :]
        else:
            nh_ref[u] = h[c - 1:c, :]
        return carry_

    lax.fori_loop(0, units, scan_body, 0)
    if carry:
        @pl.when(t == last)
        def _():
            nh_ref[0] = hcar[...]


def _lru(x, cw, cb, wa, ba, wx, bx, lam, state, *, batch, seq):
    w = LRU_WIDTH
    carry = state is None
    if carry:
        c, units = PROMPT_BLOCK, 1
        nt = seq // c
        grid = (batch, nt)
        rows = lambda b, t: (b * nt + t, 0)
        const = lambda b, t: (0, 0)
        st = lambda b, t: (b, 0, 0)
        sem = ("parallel", "arbitrary")
        state_in, state_specs = [], []
    else:
        c, units = seq, batch
        grid = (1,)
        rows = lambda i: (0, 0)
        const = lambda i: (0, 0)
        st = lambda i: (0, 0, 0)
        sem = ("arbitrary",)
        state_in = [state[0], state[1].reshape(batch, 1, w)]
        state_specs = [pl.BlockSpec((units, CONV_W - 1, w), st), pl.BlockSpec((units, 1, w), st)]
    n = batch * seq
    blk = c * units
    scratch = [pltpu.VMEM((c + SUBLANES, w), F32), pltpu.VMEM((blk, w), F32),
               pltpu.VMEM((blk, w), F32), pltpu.VMEM((blk, w), F32)]
    if carry:
        scratch.append(pltpu.VMEM((1, w), F32))
    out, nconv, nh = pl.pallas_call(
        functools.partial(_lru_kernel, c=c, units=units, carry=carry),
        grid=grid,
        in_specs=[pl.BlockSpec((blk, LRU_COLS), rows),
                  pl.BlockSpec((CONV_W, w), const),
                  pl.BlockSpec((1, w), const),
                  pl.BlockSpec((w, w), const),
                  pl.BlockSpec((1, w), const),
                  pl.BlockSpec((w, w), const),
                  pl.BlockSpec((1, w), const),
                  pl.BlockSpec((1, w), const)] + state_specs,
        out_specs=[pl.BlockSpec((blk, w), rows),
                   pl.BlockSpec((units, CONV_W - 1, w), st),
                   pl.BlockSpec((units, 1, w), st)],
        out_shape=[jax.ShapeDtypeStruct((n, w), F32),
                   jax.ShapeDtypeStruct((batch, CONV_W - 1, w), F32),
                   jax.ShapeDtypeStruct((batch, 1, w), F32)],
        scratch_shapes=scratch,
        compiler_params=_cparams(sem),
        name="lru_prompt" if carry else "lru_sample",
    )(x, cw, cb, wa, ba, wx, bx, lam, *state_in)
    return out, nconv, nh.reshape(batch, w)


def _rope(x, cos, sin):
    parts = []
    for j in range(x.shape[1] // LANES):
        sl = slice(j * LANES, (j + 1) * LANES)
        xs = x[:, sl]
        lane = lax.broadcasted_iota(jnp.int32, xs.shape, 1)
        swapped = jnp.where((lane & HALF) == 0,
                            pltpu.roll(xs, LANES - HALF, axis=1), pltpu.roll(xs, HALF, axis=1))
        parts.append(xs * cos[:, sl] + swapped * sin[:, sl])
    return jnp.concatenate(parts, axis=1)


def _ret_kernel(*refs, c, units, carry, mm):
    if carry:
        (x_ref, cos_ref, sin_ref, dmat_ref, qdec_ref, kdec_ref, gc_ref, ng_ref,
         out_ref, ns_ref, s_scr) = refs
        t = pl.program_id(1)
        last = pl.num_programs(1) - 1

        @pl.when(t == 0)
        def _():
            s_scr[...] = jnp.zeros(s_scr.shape, F32)
    else:
        (x_ref, cos_ref, sin_ref, dmat_ref, qdec_ref, kdec_ref, gc_ref, ng_ref, s0_ref,
         out_ref, ns_ref) = refs
    w = RET_WIDTH

    def body(u, carry_):
        r0 = pl.multiple_of(u * c, SUBLANES)
        rows = pl.ds(r0, c)
        cos = cos_ref[rows, :]
        sin = sin_ref[rows, :]
        q = _rope(x_ref[rows, 0:w], cos, sin) * (HEAD_DIM ** -0.5)
        k = _rope(x_ref[rows, w:2 * w], cos, sin)
        v = x_ref[rows, 2 * w:3 * w]
        qd = q * qdec_ref[...]
        kd = k * kdec_ref[...]
        outs = []
        for h in range(N_HEADS):
            sl = slice(h * HEAD_DIM, (h + 1) * HEAD_DIM)
            qh = q[:, sl].astype(mm)
            kh = k[:, sl].astype(mm)
            vh = v[:, sl].astype(mm)
            s = s_scr[h] if carry else s0_ref[u, h]
            scores = _dot_nt(qh, kh) * dmat_ref[h]
            o = _dot(scores.astype(mm), vh) + _dot(qd[:, sl].astype(mm), s.astype(mm))
            s_new = gc_ref[:, sl] * s + _dot_tn(kd[:, sl].astype(mm), vh)
            if carry:
                s_scr[h] = s_new
            else:
                ns_ref[u, h] = s_new
            outs.append(o * lax.rsqrt(jnp.mean(o * o, axis=-1, keepdims=True) + EPS))
        g = x_ref[rows, 3 * w:4 * w]
        out_ref[rows, :] = jnp.concatenate(outs, axis=1) * ng_ref[...] * _silu(g)
        return carry_

    lax.fori_loop(0, units, body, 0)
    if carry:
        @pl.when(t == last)
        def _():
            ns_ref[0] = s_scr[...]


def _ret_tables(c, pos0, seq):
    hh = np.arange(N_HEADS, dtype=np.float64)
    lg = np.log(1.0 - 2.0 ** (-5.0 - hh))
    n = np.arange(c, dtype=np.float64)
    rel = n[:, None] - n[None, :]
    dmat = np.where(rel[None] >= 0, np.exp(np.maximum(rel, 0.0)[None] * lg[:, None, None]), 0.0)
    rep = lambda a: np.repeat(a, HEAD_DIM, axis=-1)
    qdec = rep(np.exp((n + 1.0)[:, None] * lg[None, :]))
    kdec = rep(np.exp((c - 1.0 - n)[:, None] * lg[None, :]))
    gc = rep(np.exp(c * lg)[None, :])
    inv = np.exp(-math.log(ROPE_BASE) * np.arange(HALF, dtype=np.float64) / HALF)
    ang = (pos0 + np.arange(seq, dtype=np.float64))[:, None] * inv[None]
    cos = np.tile(np.concatenate([np.cos(ang), np.cos(ang)], axis=1), (1, N_HEADS))
    sin = np.tile(np.concatenate([-np.sin(ang), np.sin(ang)], axis=1), (1, N_HEADS))
    f = lambda a: jnp.asarray(a, dtype=F32)
    return f(cos), f(sin), f(dmat), f(qdec), f(kdec), f(gc)


def _ret(x, ng, state, *, batch, seq, pos0):
    w = RET_WIDTH
    carry = state is None
    if carry:
        c, units = PROMPT_BLOCK, 1
        nt = seq // c
        grid = (batch, nt)
        rows = lambda b, t: (b * nt + t, 0)
        trow = lambda b, t: (t, 0)
        const = lambda b, t: (0, 0)
        const3 = lambda b, t: (0, 0, 0)
        st = lambda b, t: (b, 0, 0, 0)
        sem = ("parallel", "arbitrary")
        cos, sin, dmat, qdec, kdec, gc = _ret_tables(c, pos0, seq)
        state_in, state_specs = [], []
        scratch = [pltpu.VMEM((N_HEADS, HEAD_DIM, HEAD_DIM), F32)]
        sblk = 1
        mm = BF16
    else:
        c, units = seq, SAMPLE_SEQS
        grid = (batch // units,)
        rows = lambda i: (i, 0)
        trow = lambda i: (0, 0)
        const = lambda i: (0, 0)
        const3 = lambda i: (0, 0, 0)
        st = lambda i: (i, 0, 0, 0)
        sem = ("parallel",)
        cos, sin, dmat, qdec, kdec, gc = _ret_tables(c, pos0, seq)
        cos = jnp.tile(cos, (units, 1))
        sin = jnp.tile(sin, (units, 1))
        state_in = [state]
        state_specs = [pl.BlockSpec((units, N_HEADS, HEAD_DIM, HEAD_DIM), st)]
        scratch = []
        sblk = units
        mm = F32
    n = batch * seq
    blk = c * units
    out, ns = pl.pallas_call(
        functools.partial(_ret_kernel, c=c, units=units, carry=carry, mm=mm),
        grid=grid,
        in_specs=[pl.BlockSpec((blk, RET_COLS), rows),
                  pl.BlockSpec((blk, w), trow),
                  pl.BlockSpec((blk, w), trow),
                  pl.BlockSpec((N_HEADS, c, c), const3),
                  pl.BlockSpec((c, w), const),
                  pl.BlockSpec((c, w), const),
                  pl.BlockSpec((1, w), const),
                  pl.BlockSpec((1, w), const)] + state_specs,
        out_specs=[pl.BlockSpec((blk, w), rows),
                   pl.BlockSpec((sblk, N_HEADS, HEAD_DIM, HEAD_DIM), st)],
        out_shape=[jax.ShapeDtypeStruct((n, w), F32),
                   jax.ShapeDtypeStruct((batch, N_HEADS, HEAD_DIM, HEAD_DIM), F32)],
        scratch_shapes=scratch,
        compiler_params=_cparams(sem),
        name="ret_prompt" if carry else "ret_sample",
    )(x, cos, sin, dmat, qdec, kdec, gc, ng, *state_in)
    return out, ns


def _unit_lower_inverse(a, c, mm):
    eye = (lax.broadcasted_iota(jnp.int32, (c, c), 0)
           == lax.broadcasted_iota(jnp.int32, (c, c), 1)).astype(F32)
    inv = eye - a
    x = a
    p = 2
    while p < c:
        xm = x.astype(mm)
        x = _dot(xm, xm)
        inv = inv + _dot(inv.astype(mm), x.astype(mm))
        p *= 2
    return inv


def _gdn_kernel(*refs, c, units, carry, mm):
    if carry:
        (x_ref, cw_ref, alog_ref, dtb_ref, ng_ref, out_ref, nconv_ref, ns_ref, ext, s_scr) = refs
        t = pl.program_id(1)
        last = pl.num_programs(1) - 1

        @pl.when(t == 0)
        def _():
            ext[0:SUBLANES, :] = jnp.zeros((SUBLANES, 3 * GDN_WIDTH), F32)
            s_scr[...] = jnp.zeros(s_scr.shape, F32)
    else:
        (x_ref, cw_ref, alog_ref, dtb_ref, ng_ref, buf_ref, s0_ref,
         out_ref, nconv_ref, ns_ref, ext) = refs
    w = GDN_WIDTH
    hist = SUBLANES - (CONV_W - 1)
    ri = lax.broadcasted_iota(jnp.int32, (c, c), 0)
    ci = lax.broadcasted_iota(jnp.int32, (c, c), 1)
    lower = ri >= ci
    strict = ri > ci
    tri_l = lower.astype(F32)
    tri_u = (ri <= ci).astype(F32)
    hi = lax.Precision.HIGHEST

    def body(u, carry_):
        r0 = pl.multiple_of(u * c, SUBLANES)
        rows = pl.ds(r0, c)
        if not carry:
            ext[hist:SUBLANES, :] = buf_ref[u]
        qkv = _silu(_conv_unit(ext, x_ref[rows, 0:3 * w], cw_ref, c))
        if carry:
            @pl.when((t == last) & (u == units - 1))
            def _():
                nconv_ref[0] = ext[c + hist:c + SUBLANES, :]
            ext[0:SUBLANES, :] = ext[c:c + SUBLANES, :]
        else:
            nconv_ref[u] = ext[c + hist:c + SUBLANES, :]
        ab = x_ref[rows, 4 * w:4 * w + LANES]
        g_log = -jnp.exp(alog_ref[...]) * _softplus(ab + dtb_ref[...])
        beta = jax.nn.sigmoid(ab)
        g_col = _dot(tri_l, g_log, precision=hi)
        g_row = _dot_tn(g_log, tri_u, precision=hi)
        outs = []
        for h in range(N_HEADS):
            sl = slice(h * HEAD_DIM, (h + 1) * HEAD_DIM)
            qr = qkv[:, sl]
            kr = qkv[:, w + h * HEAD_DIM:w + (h + 1) * HEAD_DIM]
            vh = qkv[:, 2 * w + h * HEAD_DIM:2 * w + (h + 1) * HEAD_DIM]
            qh = qr * lax.rsqrt(jnp.sum(qr * qr, axis=-1, keepdims=True) + EPS) * (HEAD_DIM ** -0.5)
            kh = kr * lax.rsqrt(jnp.sum(kr * kr, axis=-1, keepdims=True) + EPS)
            gi = g_col[:, h:h + 1]
            gj = g_row[h:h + 1, :]
            bi = beta[:, GDN_BETA_LANE + h:GDN_BETA_LANE + h + 1]
            decay = jnp.exp(jnp.where(lower, gi - gj, -jnp.inf))
            kb = kh.astype(mm)
            a_mat = jnp.where(strict, bi * _dot_nt(kb, kb) * decay, 0.0)
            tm = _unit_lower_inverse(a_mat, c, mm)
            eg = jnp.exp(gi)
            rhs = jnp.concatenate([bi * vh, (bi * eg) * kh], axis=1)
            uw = _dot(tm.astype(mm), rhs.astype(mm))
            uu = uw[:, 0:HEAD_DIM]
            ww = uw[:, HEAD_DIM:2 * HEAD_DIM]
            pm = _dot_nt(qh.astype(mm), kb) * decay
            g_last = g_col[c - 1:c, h:h + 1]
            s = s_scr[h] if carry else s0_ref[u, h]
            sm = s.astype(mm)
            ws = _dot(ww.astype(mm), sm)
            qs = _dot((qh * eg).astype(mm), sm)
            vn = uu - ws
            vnm = vn.astype(mm)
            o = qs + _dot(pm.astype(mm), vnm)
            s_new = jnp.exp(g_last) * s + _dot_tn((kh * jnp.exp(g_last - gi)).astype(mm), vnm)
            if carry:
                s_scr[h] = s_new
            else:
                ns_ref[u, h] = s_new
            outs.append(o * lax.rsqrt(jnp.mean(o * o, axis=-1, keepdims=True) + EPS) * ng_ref[...])
        z = x_ref[rows, 3 * w:4 * w]
        out_ref[rows, :] = jnp.concatenate(outs, axis=1) * _silu(z)
        return carry_

    lax.fori_loop(0, units, body, 0)
    if carry:
        @pl.when(t == last)
        def _():
            ns_ref[0] = s_scr[...]


def _gdn(x, cw, alog, dtb, ng, state, *, batch, seq):
    w = GDN_WIDTH
    carry = state is None
    c = math.gcd(seq, CHUNK)
    if carry:
        units = PROMPT_BLOCK // c
        nt = seq // (c * units)
        grid = (batch, nt)
        rows = lambda b, t: (b * nt + t, 0)
        const = lambda b, t: (0, 0)
        st3 = lambda b, t: (b, 0, 0)
        st4 = lambda b, t: (b, 0, 0, 0)
        sem = ("parallel", "arbitrary")
        state_in, state_specs = [], []
        scratch = [pltpu.VMEM((c + SUBLANES, 3 * w), F32),
                   pltpu.VMEM((N_HEADS, HEAD_DIM, HEAD_DIM), F32)]
        sblk = 1
        mm = BF16
    else:
        units = SAMPLE_SEQS
        grid = (batch // units,)
        rows = lambda i: (i, 0)
        const = lambda i: (0, 0)
        st3 = lambda i: (i, 0, 0)
        st4 = lambda i: (i, 0, 0, 0)
        sem = ("parallel",)
        state_in = list(state)
        state_specs = [pl.BlockSpec((units, CONV_W - 1, 3 * w), st3),
                       pl.BlockSpec((units, N_HEADS, HEAD_DIM, HEAD_DIM), st4)]
        scratch = [pltpu.VMEM((c + SUBLANES, 3 * w), F32)]
        sblk = units
        mm = F32
    n = batch * seq
    blk = c * units
    out, nconv, ns = pl.pallas_call(
        functools.partial(_gdn_kernel, c=c, units=units, carry=carry, mm=mm),
        grid=grid,
        in_specs=[pl.BlockSpec((blk, GDN_COLS), rows),
                  pl.BlockSpec((CONV_W, 3 * w), const),
                  pl.BlockSpec((1, LANES), const),
                  pl.BlockSpec((1, LANES), const),
                  pl.BlockSpec((1, HEAD_DIM), const)] + state_specs,
        out_specs=[pl.BlockSpec((blk, w), rows),
                   pl.BlockSpec((sblk, CONV_W - 1, 3 * w), st3),
                   pl.BlockSpec((sblk, N_HEADS, HEAD_DIM, HEAD_DIM), st4)],
        out_shape=[jax.ShapeDtypeStruct((n, w), F32),
                   jax.ShapeDtypeStruct((batch, CONV_W - 1, 3 * w), F32),
                   jax.ShapeDtypeStruct((batch, N_HEADS, HEAD_DIM, HEAD_DIM), F32)],
        scratch_shapes=scratch,
        compiler_params=_cparams(sem),
        name="gdn_prompt" if carry else "gdn_sample",
    )(x, cw, alog, dtb, ng, *state_in)
    return out, nconv, ns


def _block_diag(wb):
    out = jnp.zeros((LRU_WIDTH, LRU_WIDTH), wb.dtype)
    for n in range(LRU_BLOCKS):
        s = slice(n * LRU_BLOCK, (n + 1) * LRU_BLOCK)
        out = out.at[s, s].set(wb[n])
    return out


def _pad_lanes(v, offset=0):
    return jnp.zeros((1, LANES), F32).at[0, offset:offset + v.shape[0]].set(v.astype(F32))


def _prep_layer(W, l):
    row = lambda v: v.reshape(1, -1).astype(F32)
    pad = IN_COLS - W['w_in'].shape[2]
    wr = jnp.zeros((D_MODEL, LANES), F32)
    wr = wr.at[:, ROUTER_GROUP_LANE:ROUTER_GROUP_LANE + N_GROUPS].set(W['w_router_group'][l])
    wr = wr.at[:, ROUTER_EXPERT_LANE:ROUTER_EXPERT_LANE + N_EXPERTS].set(W['w_router_expert'][l])
    br = jnp.zeros((1, LANES), F32)
    br = br.at[0, ROUTER_GROUP_LANE:ROUTER_GROUP_LANE + N_GROUPS].set(W['b_router_group'][l])
    br = br.at[0, ROUTER_EXPERT_LANE:ROUTER_EXPERT_LANE + N_EXPERTS].set(W['b_router_expert'][l])
    return dict(
        norm_mix_g=row(W['norm_mix_g'][l]),
        w_in=jnp.pad(W['w_in'][l], ((0, 0), (0, pad))).astype(BF16),
        lru_conv_w=W['lru_conv_w'][l], lru_conv_b=row(W['lru_conv_b'][l]),
        lru_wa=_block_diag(W['lru_wa'][l]).astype(BF16), lru_ba=row(W['lru_ba'][l]),
        lru_wx=_block_diag(W['lru_wx'][l]).astype(BF16), lru_bx=row(W['lru_bx'][l]),
        lru_lambda=row(W['lru_lambda'][l]),
        ret_norm_g=row(W['ret_norm_g'][l]),
        gdn_conv_w=W['gdn_conv_w'][l],
        gdn_a_log=_pad_lanes(W['gdn_a_log'][l]), gdn_dt_bias=_pad_lanes(W['gdn_dt_bias'][l]),
        gdn_norm_g=row(W['gdn_norm_g'][l]),
        w_out=W['w_out'][l].astype(BF16),
        norm_ffn_g=row(W['norm_ffn_g'][l]),
        w_router=wr, b_router=br,
        w_expert_gate=W['w_expert_gate'][l].astype(BF16),
        w_expert_up=W['w_expert_up'][l].astype(BF16),
        w_expert_down=W['w_expert_down'][l].astype(BF16),
        norm_pe_g=row(W['norm_pe_g'][l]),
        w_pe=W['w_pe'][l].astype(BF16),
        w_pe_gate=W['w_pe_gate'][l].astype(BF16),
    )


def _trunk(x, p, states, pos0, layers, final_g):
    batch, seq, _ = x.shape
    n = batch * seq
    h = x.reshape(n, D_MODEL)
    new = ([], [], [], [], [])
    for l, L in enumerate(layers):
        if states is None:
            st_lru = st_ret = st_gdn = None
        else:
            st_lru = (states[0][l], states[1][l])
            st_ret = states[2][l]
            st_gdn = (states[3][l], states[4][l])
        lru_x, ret_x, gdn_x = _in_proj(h, L['norm_mix_g'], L['w_in'])
        out_a, n_lru_conv, n_lru_h = _lru(
            lru_x, L['lru_conv_w'], L['lru_conv_b'], L['lru_wa'], L['lru_ba'], L['lru_wx'],
            L['lru_bx'], L['lru_lambda'], st_lru, batch=batch, seq=seq)
        out_b, n_ret = _ret(ret_x, L['ret_norm_g'], st_ret, batch=batch, seq=seq, pos0=pos0)
        out_c, n_gdn_conv, n_gdn = _gdn(
            gdn_x, L['gdn_conv_w'], L['gdn_a_log'], L['gdn_dt_bias'], L['gdn_norm_g'], st_gdn,
            batch=batch, seq=seq)
        h1, xn, gates = _out_router(h, out_a, out_b, out_c, L['w_out'], L['norm_ffn_g'],
                                    L['w_router'], L['b_router'])
        h2 = _moe(xn, gates, h1, L['w_expert_gate'], L['w_expert_up'], L['w_expert_down'])
        h = _pe(h2, p[l].reshape(n, PE_DIM), L['norm_pe_g'], L['w_pe_gate'], L['w_pe'], final_g,
                final=(l == len(layers) - 1))
        for lst, s in zip(new, (n_lru_conv, n_lru_h, n_ret, n_gdn_conv, n_gdn)):
            lst.append(s)
    return h.reshape(batch, seq, D_MODEL), tuple(jnp.stack(lst) for lst in new)


def kernel(x_prompt, x_sample, p_prompt, p_sample, state_lru_conv, state_lru_h, state_ret, state_gdn_conv, state_gdn, norm_mix_g, w_in, lru_conv_w, lru_conv_b, lru_wa, lru_ba, lru_wx, lru_bx, lru_lambda, ret_norm_g, gdn_conv_w, gdn_a_log, gdn_dt_bias, gdn_norm_g, w_out, norm_ffn_g, w_router_group, b_router_group, w_router_expert, b_router_expert, w_expert_gate, w_expert_up, w_expert_down, norm_pe_g, w_pe, w_pe_gate, final_norm_g):
    W = dict(norm_mix_g=norm_mix_g, w_in=w_in, lru_conv_w=lru_conv_w, lru_conv_b=lru_conv_b,
             lru_wa=lru_wa, lru_ba=lru_ba, lru_wx=lru_wx, lru_bx=lru_bx, lru_lambda=lru_lambda,
             ret_norm_g=ret_norm_g, gdn_conv_w=gdn_conv_w, gdn_a_log=gdn_a_log, gdn_dt_bias=gdn_dt_bias,
             gdn_norm_g=gdn_norm_g, w_out=w_out, norm_ffn_g=norm_ffn_g, w_router_group=w_router_group,
             b_router_group=b_router_group, w_router_expert=w_router_expert, b_router_expert=b_router_expert,
             w_expert_gate=w_expert_gate, w_expert_up=w_expert_up, w_expert_down=w_expert_down,
             norm_pe_g=norm_pe_g, w_pe=w_pe, w_pe_gate=w_pe_gate)
    layers = [_prep_layer(W, l) for l in range(DEPTH)]
    final_g = final_norm_g.reshape(1, D_MODEL).astype(F32)
    y_p, st_p = _trunk(x_prompt, p_prompt, None, 0, layers, final_g)
    sample_states = (state_lru_conv, state_lru_h, state_ret, state_gdn_conv, state_gdn)
    y_s, st_s = _trunk(x_sample, p_sample, sample_states, PAST_LEN, layers, final_g)
    return (y_p, y_s) + st_p + st_s
```

```python
import functools
import math

import jax
import jax.numpy as jnp
import numpy as np
from jax import lax
from jax.experimental import pallas as pl
from jax.experimental.pallas import tpu as pltpu

F32 = jnp.float32
BF16 = jnp.bfloat16

D_MODEL = 1024
DEPTH = 2
PAST_LEN = 16384
HEAD_DIM = 64
HALF = HEAD_DIM // 2
LRU_WIDTH = 256
LRU_BLOCKS = 4
LRU_BLOCK = 64
LRU_C = 8.0
RET_WIDTH = 384
GDN_WIDTH = 384
N_HEADS = 6
CONV_W = 4
CHUNK = 64
ROPE_BASE = 10000.0
PE_DIM = 256
N_GROUPS = 4
EXPERTS_PER_GROUP = 4
N_EXPERTS = 16
D_EXPERT = 256
EPS = 1e-6

LANES = 128
SUBLANES = 8
VMEM_LIMIT = 48 * 1024 * 1024

LRU_COLS = 2 * LRU_WIDTH
RET_COLS = 4 * RET_WIDTH
GDN_COLS = 4 * GDN_WIDTH + LANES
IN_COLS = LRU_COLS + RET_COLS + GDN_COLS
ROUTER_GROUP_LANE = 0
ROUTER_EXPERT_LANE = N_GROUPS
GDN_BETA_LANE = N_HEADS

TOKEN_TILE = 256
MOE_TILE = 1024
PROMPT_BLOCK = 256
SAMPLE_SEQS = 16


def _cparams(sem):
    return pltpu.CompilerParams(dimension_semantics=sem, vmem_limit_bytes=VMEM_LIMIT)


def _rmsnorm(x, g):
    return x * lax.rsqrt(jnp.mean(x * x, axis=-1, keepdims=True) + EPS) * g


def _dot(a, b, **kw):
    return jnp.dot(a, b, preferred_element_type=F32, **kw)


def _dot_nt(a, b):
    return lax.dot_general(a, b, (((1,), (1,)), ((), ())), preferred_element_type=F32)


def _dot_tn(a, b, **kw):
    return lax.dot_general(a, b, (((0,), (0,)), ((), ())), preferred_element_type=F32, **kw)


def _bmm(spec, a, b):
    return jnp.einsum(spec, a, b, preferred_element_type=F32)


def _bmm3(spec, a, b):
    a_hi = a.astype(BF16)
    a_lo = (a - a_hi.astype(F32)).astype(BF16)
    b_hi = b.astype(BF16)
    b_lo = (b - b_hi.astype(F32)).astype(BF16)
    return _bmm(spec, a_hi, b_hi) + (_bmm(spec, a_hi, b_lo) + _bmm(spec, a_lo, b_hi))


def _softplus(x):
    return jnp.maximum(x, 0.0) + jnp.log1p(jnp.exp(-jnp.abs(x)))


def _silu(x):
    return x * jax.nn.sigmoid(x)


def _gelu_tanh(x):
    return 0.5 * x * (1.0 + jnp.tanh(math.sqrt(2.0 / math.pi) * (x + 0.044715 * (x * x * x))))


def _in_proj_kernel(h_ref, g_ref, w_ref, lru_ref, ret_ref, gdn_ref):
    xn = _rmsnorm(h_ref[...], g_ref[...]).astype(BF16)
    lru_ref[...] = _dot(xn, w_ref[:, 0:LRU_COLS])
    ret_ref[...] = _dot(xn, w_ref[:, LRU_COLS:LRU_COLS + RET_COLS])
    gdn_ref[...] = _dot(xn, w_ref[:, LRU_COLS + RET_COLS:IN_COLS])


def _in_proj(h, g, w):
    n = h.shape[0]
    tm = TOKEN_TILE
    row = lambda i: (i, 0)
    const = lambda i: (0, 0)
    return pl.pallas_call(
        _in_proj_kernel,
        grid=(n // tm,),
        in_specs=[pl.BlockSpec((tm, D_MODEL), row),
                  pl.BlockSpec((1, D_MODEL), const),
                  pl.BlockSpec((D_MODEL, IN_COLS), const)],
        out_specs=[pl.BlockSpec((tm, LRU_COLS), row),
                   pl.BlockSpec((tm, RET_COLS), row),
                   pl.BlockSpec((tm, GDN_COLS), row)],
        out_shape=[jax.ShapeDtypeStruct((n, LRU_COLS), F32),
                   jax.ShapeDtypeStruct((n, RET_COLS), F32),
                   jax.ShapeDtypeStruct((n, GDN_COLS), F32)],
        compiler_params=_cparams(("parallel",)),
        name="in_proj",
    )(h, g, w)


def _route(logits):
    lane = lax.broadcasted_iota(jnp.int32, logits.shape, 1).astype(F32)
    neg = -jnp.inf
    far = float(LANES)
    gmask = lane < N_GROUPS
    gl = jnp.where(gmask, logits, neg)
    gmax = jnp.max(gl, axis=-1, keepdims=True)
    gsel = jnp.min(jnp.where(gl == gmax, lane, far), axis=-1, keepdims=True)
    gsum = jnp.sum(jnp.where(gmask, jnp.exp(logits - gmax), 0.0), axis=-1, keepdims=True)
    g_w = 1.0 / gsum
    lo = ROUTER_EXPERT_LANE + EXPERTS_PER_GROUP * gsel
    el = jnp.where(lane >= lo, jnp.where(lane < lo + EXPERTS_PER_GROUP, logits, neg), neg)
    m1 = jnp.max(el, axis=-1, keepdims=True)
    i1 = jnp.min(jnp.where(el == m1, lane, far), axis=-1, keepdims=True)
    el2 = jnp.where(lane == i1, neg, el)
    m2 = jnp.max(el2, axis=-1, keepdims=True)
    i2 = jnp.min(jnp.where(el2 == m2, lane, far), axis=-1, keepdims=True)
    e2 = jnp.exp(m2 - m1)
    w1 = g_w / (1.0 + e2)
    w2 = w1 * e2
    return jnp.where(lane == i1, w1, 0.0) + jnp.where(lane == i2, w2, 0.0)


def _out_router_kernel(h_ref, a_ref, b_ref, c_ref, wo_ref, g_ref, wr_ref, br_ref,
                       h1_ref, xn_ref, gates_ref):
    o1 = LRU_WIDTH
    o2 = LRU_WIDTH + RET_WIDTH
    mix = (_dot(a_ref[...].astype(BF16), wo_ref[0:o1, :])
           + _dot(b_ref[...].astype(BF16), wo_ref[o1:o2, :])
           + _dot(c_ref[...].astype(BF16), wo_ref[o2:D_MODEL, :]))
    h1 = h_ref[...] + mix
    h1_ref[...] = h1
    xn = _rmsnorm(h1, g_ref[...]).astype(BF16)
    xn_ref[...] = xn
    gates_ref[...] = _route(_dot(xn, wr_ref[...]) + br_ref[...])


def _out_router(h, a, b, c, wo, g, wr, br):
    n = h.shape[0]
    tm = TOKEN_TILE
    row = lambda i: (i, 0)
    const = lambda i: (0, 0)
    return pl.pallas_call(
        _out_router_kernel,
        grid=(n // tm,),
        in_specs=[pl.BlockSpec((tm, D_MODEL), row),
                  pl.BlockSpec((tm, LRU_WIDTH), row),
                  pl.BlockSpec((tm, RET_WIDTH), row),
                  pl.BlockSpec((tm, GDN_WIDTH), row),
                  pl.BlockSpec((D_MODEL, D_MODEL), const),
                  pl.BlockSpec((1, D_MODEL), const),
                  pl.BlockSpec((D_MODEL, LANES), const),
                  pl.BlockSpec((1, LANES), const)],
        out_specs=[pl.BlockSpec((tm, D_MODEL), row),
                   pl.BlockSpec((tm, D_MODEL), row),
                   pl.BlockSpec((tm, LANES), row)],
        out_shape=[jax.ShapeDtypeStruct((n, D_MODEL), F32),
                   jax.ShapeDtypeStruct((n, D_MODEL), BF16),
                   jax.ShapeDtypeStruct((n, LANES), F32)],
        compiler_params=_cparams(("parallel",)),
        name="out_router",
    )(h, a, b, c, wo, g, wr, br)


def _moe_kernel(xn_ref, gates_ref, h1_ref, wg_ref, wu_ref, wd_ref, out_ref):
    e = pl.program_id(1)

    @pl.when(e == 0)
    def _():
        out_ref[...] = h1_ref[...]

    xn = xn_ref[...]
    hg = _dot(xn, wg_ref[...])
    hu = _dot(xn, wu_ref[...])
    gates = gates_ref[...]
    lane = lax.broadcasted_iota(jnp.int32, gates.shape, 1)
    gcol = jnp.sum(jnp.where(lane == e + ROUTER_EXPERT_LANE, gates, 0.0), axis=-1, keepdims=True)
    hmid = (_silu(hg) * hu * gcol).astype(BF16)
    out_ref[...] += _dot(hmid, wd_ref[...])


def _moe(xn, gates, h1, wg, wu, wd):
    n = xn.shape[0]
    tm = MOE_TILE if n % MOE_TILE == 0 else TOKEN_TILE
    row = lambda i, e: (i, 0)
    return pl.pallas_call(
        _moe_kernel,
        grid=(n // tm, N_EXPERTS),
        in_specs=[pl.BlockSpec((tm, D_MODEL), row),
                  pl.BlockSpec((tm, LANES), row),
                  pl.BlockSpec((tm, D_MODEL), row),
                  pl.BlockSpec((None, D_MODEL, D_EXPERT), lambda i, e: (e, 0, 0)),
                  pl.BlockSpec((None, D_MODEL, D_EXPERT), lambda i, e: (e, 0, 0)),
                  pl.BlockSpec((None, D_EXPERT, D_MODEL), lambda i, e: (e, 0, 0))],
        out_specs=pl.BlockSpec((tm, D_MODEL), row),
        out_shape=jax.ShapeDtypeStruct((n, D_MODEL), F32),
        compiler_params=_cparams(("parallel", "arbitrary")),
        name="moe",
    )(xn, gates, h1, wg, wu, wd)


def _pe_kernel(h_ref, p_ref, g_ref, wgate_ref, wpe_ref, fg_ref, out_ref, *, final):
    h = h_ref[...]
    xn = _rmsnorm(h, g_ref[...]).astype(BF16)
    gate = jax.nn.sigmoid(_dot(xn, wgate_ref[...]))
    pe = _dot(p_ref[...].astype(BF16), wpe_ref[...])
    h3 = h + pe * gate
    out_ref[...] = _rmsnorm(h3, fg_ref[...]) if final else h3


def _pe(h, p, g, wgate, wpe, fg, final):
    n = h.shape[0]
    tm = TOKEN_TILE
    row = lambda i: (i, 0)
    const = lambda i: (0, 0)
    return pl.pallas_call(
        functools.partial(_pe_kernel, final=final),
        grid=(n // tm,),
        in_specs=[pl.BlockSpec((tm, D_MODEL), row),
                  pl.BlockSpec((tm, PE_DIM), row),
                  pl.BlockSpec((1, D_MODEL), const),
                  pl.BlockSpec((D_MODEL, D_MODEL), const),
                  pl.BlockSpec((PE_DIM, D_MODEL), const),
                  pl.BlockSpec((1, D_MODEL), const)],
        out_specs=pl.BlockSpec((tm, D_MODEL), row),
        out_shape=jax.ShapeDtypeStruct((n, D_MODEL), F32),
        compiler_params=_cparams(("parallel",)),
        name="pe_gate",
    )(h, p, g, wgate, wpe, fg)


def _conv_unit(ext, x, w_ref, c):
    ext[SUBLANES:SUBLANES + c, :] = x
    y = w_ref[CONV_W - 1:CONV_W, :] * x
    for j in range(CONV_W - 1):
        off = SUBLANES - (CONV_W - 1) + j
        y = y + w_ref[j:j + 1, :] * ext[off:off + c, :]
    return y


def _lru_kernel(*refs, c, units, carry):
    if carry:
        (x_ref, cw_ref, cb_ref, wa_ref, ba_ref, wx_ref, bx_ref, lam_ref,
         out_ref, nconv_ref, nh_ref, ext, y_scr, a_scr, b_scr, hcar) = refs
    else:
        (x_ref, cw_ref, cb_ref, wa_ref, ba_ref, wx_ref, bx_ref, lam_ref, buf_ref, h0_ref,
         out_ref, nconv_ref, nh_ref, ext, y_scr, a_scr, b_scr) = refs
    w = LRU_WIDTH
    hist = SUBLANES - (CONV_W - 1)
    if carry:
        t = pl.program_id(1)
        last = pl.num_programs(1) - 1

        @pl.when(t == 0)
        def _():
            ext[0:SUBLANES, :] = jnp.zeros((SUBLANES, w), F32)
            hcar[...] = jnp.zeros((1, w), F32)

    def conv_body(u, carry_):
        r0 = pl.multiple_of(u * c, SUBLANES)
        if not carry:
            ext[hist:SUBLANES, :] = buf_ref[u]
        x = x_ref[pl.ds(r0, c), 0:w]
        y_scr[pl.ds(r0, c), :] = _conv_unit(ext, x, cw_ref, c) + cb_ref[...]
        if carry:
            @pl.when(t == last)
            def _():
                nconv_ref[0] = ext[c + hist:c + SUBLANES, :]
            ext[0:SUBLANES, :] = ext[c:c + SUBLANES, :]
        else:
            nconv_ref[u] = ext[c + hist:c + SUBLANES, :]
        return carry_

    lax.fori_loop(0, units, conv_body, 0)

    y = y_scr[...]
    yb = y.astype(BF16)
    r = jax.nn.sigmoid(_dot(yb, wa_ref[...]) + ba_ref[...])
    i = jax.nn.sigmoid(_dot(yb, wx_ref[...]) + bx_ref[...])
    log_a = (-LRU_C) * r * _softplus(-lam_ref[...])
    th = jnp.tanh(log_a)
    a_scr[...] = jnp.exp(log_a)
    b_scr[...] = jnp.sqrt(-2.0 * th / (1.0 - th)) * (i * y)

    def scan_body(u, carry_):
        r0 = pl.multiple_of(u * c, SUBLANES)
        a = a_scr[pl.ds(r0, c), :]
        b = b_scr[pl.ds(r0, c), :]
        rowi = lax.broadcasted_iota(jnp.int32, (c, w), 0)
        d = 1
        while d < c:
            keep = rowi >= d
            a_s = pltpu.roll(a, d, axis=0)
            b_s = pltpu.roll(b, d, axis=0)
            b = jnp.where(keep, a * b_s + b, b)
            a = jnp.where(keep, a * a_s, a)
            d *= 2
        h0 = hcar[...] if carry else h0_ref[u]
        h = a * h0 + b
        gate = x_ref[pl.ds(r0, c), w:2 * w]
        out_ref[pl.ds(r0, c), :] = h * _gelu_tanh(gate)
        if carry:
            hcar[...] = h[c - 1:c, :]
        else:
            nh_ref[u] = h[c - 1:c, :]
        return carry_

    lax.fori_loop(0, units, scan_body, 0)
    if carry:
        @pl.when(t == last)
        def _():
            nh_ref[0] = hcar[...]


def _lru(x, cw, cb, wa, ba, wx, bx, lam, state, *, batch, seq):
    w = LRU_WIDTH
    carry = state is None
    if carry:
        c, units = PROMPT_BLOCK, 1
        nt = seq // c
        grid = (batch, nt)
        rows = lambda b, t: (b * nt + t, 0)
        const = lambda b, t: (0, 0)
        st = lambda b, t: (b, 0, 0)
        sem = ("parallel", "arbitrary")
        state_in, state_specs = [], []
    else:
        c, units = seq, batch
        grid = (1,)
        rows = lambda i: (0, 0)
        const = lambda i: (0, 0)
        st = lambda i: (0, 0, 0)
        sem = ("arbitrary",)
        state_in = [state[0], state[1].reshape(batch, 1, w)]
        state_specs = [pl.BlockSpec((units, CONV_W - 1, w), st), pl.BlockSpec((units, 1, w), st)]
    n = batch * seq
    blk = c * units
    scratch = [pltpu.VMEM((c + SUBLANES, w), F32), pltpu.VMEM((blk, w), F32),
               pltpu.VMEM((blk, w), F32), pltpu.VMEM((blk, w), F32)]
    if carry:
        scratch.append(pltpu.VMEM((1, w), F32))
    out, nconv, nh = pl.pallas_call(
        functools.partial(_lru_kernel, c=c, units=units, carry=carry),
        grid=grid,
        in_specs=[pl.BlockSpec((blk, LRU_COLS), rows),
                  pl.BlockSpec((CONV_W, w), const),
                  pl.BlockSpec((1, w), const),
                  pl.BlockSpec((w, w), const),
                  pl.BlockSpec((1, w), const),
                  pl.BlockSpec((w, w), const),
                  pl.BlockSpec((1, w), const),
                  pl.BlockSpec((1, w), const)] + state_specs,
        out_specs=[pl.BlockSpec((blk, w), rows),
                   pl.BlockSpec((units, CONV_W - 1, w), st),
                   pl.BlockSpec((units, 1, w), st)],
        out_shape=[jax.ShapeDtypeStruct((n, w), F32),
                   jax.ShapeDtypeStruct((batch, CONV_W - 1, w), F32),
                   jax.ShapeDtypeStruct((batch, 1, w), F32)],
        scratch_shapes=scratch,
        compiler_params=_cparams(sem),
        name="lru_prompt" if carry else "lru_sample",
    )(x, cw, cb, wa, ba, wx, bx, lam, *state_in)
    return out, nconv, nh.reshape(batch, w)


def _rope(x, cos, sin):
    parts = []
    for j in range(x.shape[1] // LANES):
        sl = slice(j * LANES, (j + 1) * LANES)
        xs = x[:, sl]
        lane = lax.broadcasted_iota(jnp.int32, xs.shape, 1)
        swapped = jnp.where((lane & HALF) == 0,
                            pltpu.roll(xs, LANES - HALF, axis=1), pltpu.roll(xs, HALF, axis=1))
        parts.append(xs * cos[:, sl] + swapped * sin[:, sl])
    return jnp.concatenate(parts, axis=1)


def _segment_masks(r, c):
    ri = lax.broadcasted_iota(jnp.int32, (r, r), 0)
    ci = lax.broadcasted_iota(jnp.int32, (r, r), 1)
    if c == r:
        return ri >= ci, ri > ci, ri <= ci
    shift = c.bit_length() - 1
    same = lax.shift_right_logical(ri, shift) == lax.shift_right_logical(ci, shift)
    return same & (ri >= ci), same & (ri > ci), same & (ri <= ci)


def _head_slice(h, base=0):
    return slice(base + h * HEAD_DIM, base + (h + 1) * HEAD_DIM)


def _head_rms(o):
    return o * lax.rsqrt(jnp.mean(o * o, axis=-1, keepdims=True) + EPS)


def _ret_prompt_kernel(x_ref, cos_ref, sin_ref, dmat_ref, qdec_ref, kdec_ref, gc_ref, ng_ref,
                       out_ref, ns_ref, s_scr, *, r, c):
    t = pl.program_id(1)
    last = pl.num_programs(1) - 1

    @pl.when(t == 0)
    def _():
        s_scr[...] = jnp.zeros(s_scr.shape, F32)

    w = RET_WIDTH
    nc = r // c
    cos = cos_ref[...]
    sin = sin_ref[...]
    q = _rope(x_ref[:, 0:w], cos, sin) * (HEAD_DIM ** -0.5)
    k = _rope(x_ref[:, w:2 * w], cos, sin)
    v = x_ref[:, 2 * w:3 * w]
    qd = q * qdec_ref[...]
    kd = k * kdec_ref[...]
    rows = lambda n: slice(n * c, (n + 1) * c)

    def stack(a):
        return jnp.stack([a[rows(n), _head_slice(h)] for n in range(nc) for h in range(N_HEADS)], axis=0)

    qs, ks, vs = stack(q).astype(BF16), stack(k).astype(BF16), stack(v).astype(BF16)
    scores = _bmm('bid,bjd->bij', qs, ks) * dmat_ref[...]
    o_inner = _bmm('bij,bje->bie', scores.astype(BF16), vs)
    kv = _bmm('bid,bie->bde', stack(kd).astype(BF16), vs)
    gc = jnp.stack([gc_ref[:, _head_slice(h)] for h in range(N_HEADS)], axis=0)
    s = s_scr[...]
    starts = []
    for n in range(nc):
        starts.append(s)
        s = gc * s + kv[n * N_HEADS:(n + 1) * N_HEADS]
    s_scr[...] = s

    @pl.when(t == last)
    def _():
        ns_ref[0] = s

    o = o_inner + _bmm('bid,bde->bie', stack(qd).astype(BF16), jnp.concatenate(starts, axis=0).astype(BF16))
    on = _head_rms(o)
    g = x_ref[:, 3 * w:4 * w]
    for n in range(nc):
        for h in range(N_HEADS):
            hs = _head_slice(h)
            out_ref[rows(n), hs] = on[n * N_HEADS + h] * ng_ref[:, hs] * _silu(g[rows(n), hs])


def _ret_sample_kernel(x_ref, cos_ref, sin_ref, dmat_ref, qdec_ref, kdec_ref, gc_ref, ng_ref, s0_ref,
                       out_ref, ns_ref, *, nseq, c):
    w = RET_WIDTH
    cos = cos_ref[...]
    sin = sin_ref[...]
    q = _rope(x_ref[:, 0:w], cos, sin) * (HEAD_DIM ** -0.5)
    k = _rope(x_ref[:, w:2 * w], cos, sin)
    v = x_ref[:, 2 * w:3 * w]
    qd = q * qdec_ref[...]
    kd = k * kdec_ref[...]
    g = x_ref[:, 3 * w:4 * w]
    per_seq = lambda a: a.reshape(nseq, c, HEAD_DIM)
    heads = range(N_HEADS)
    vb = [v[:, _head_slice(h)].astype(BF16) for h in heads]
    scores = [_dot_nt(q[:, _head_slice(h)].astype(BF16), k[:, _head_slice(h)].astype(BF16)) * dmat_ref[h]
              for h in heads]
    o_inner = [_dot(scores[h].astype(BF16), vb[h]) for h in heads]
    s0 = [s0_ref[:, h] for h in heads]
    o_cross = [_bmm('bqd,bde->bqe', per_seq(qd[:, _head_slice(h)]), s0[h]) for h in heads]
    kv = [_bmm('bkd,bke->bde', per_seq(kd[:, _head_slice(h)]), per_seq(v[:, _head_slice(h)])) for h in heads]
    for h in heads:
        hs = _head_slice(h)
        ns_ref[:, h] = gc_ref[:, hs] * s0[h] + kv[h]
        o = o_inner[h] + o_cross[h].reshape(nseq * c, HEAD_DIM)
        out_ref[:, hs] = _head_rms(o) * ng_ref[:, hs] * _silu(g[:, hs])


def _rope_tables(pos0, seq):
    inv = np.exp(-math.log(ROPE_BASE) * np.arange(HALF, dtype=np.float64) / HALF)
    ang = (pos0 + np.arange(seq, dtype=np.float64))[:, None] * inv[None]
    cos = np.tile(np.concatenate([np.cos(ang), np.cos(ang)], axis=1), (1, N_HEADS))
    sin = np.tile(np.concatenate([-np.sin(ang), np.sin(ang)], axis=1), (1, N_HEADS))
    return jnp.asarray(cos, dtype=F32), jnp.asarray(sin, dtype=F32)


def _ret_decay_tables(c):
    hh = np.arange(N_HEADS, dtype=np.float64)
    lg = np.log(1.0 - 2.0 ** (-5.0 - hh))
    n = np.arange(c, dtype=np.float64)
    rel = n[:, None] - n[None, :]
    dmat = np.where(rel[None] >= 0, np.exp(np.maximum(rel, 0.0)[None] * lg[:, None, None]), 0.0)
    rep = lambda a: np.repeat(a, HEAD_DIM, axis=-1)
    qdec = rep(np.exp((n + 1.0)[:, None] * lg[None, :]))
    kdec = rep(np.exp((c - 1.0 - n)[:, None] * lg[None, :]))
    gc = rep(np.exp(c * lg)[None, :])
    return dmat, qdec, kdec, gc


def _ret(x, ng, state, *, batch, seq, pos0):
    w = RET_WIDTH
    n = batch * seq
    c = math.gcd(seq, CHUNK)
    dmat, qdec, kdec, gc = _ret_decay_tables(c)
    cos, sin = _rope_tables(pos0, seq)
    f = lambda a: jnp.asarray(a, dtype=F32)
    if state is None:
        r = PROMPT_BLOCK
        nc = r // c
        nt = seq // r
        rows = lambda b, t: (b * nt + t, 0)
        trow = lambda b, t: (t, 0)
        const = lambda b, t: (0, 0)
        const3 = lambda b, t: (0, 0, 0)
        st = lambda b, t: (b, 0, 0, 0)
        out, ns = pl.pallas_call(
            functools.partial(_ret_prompt_kernel, r=r, c=c),
            grid=(batch, nt),
            in_specs=[pl.BlockSpec((r, RET_COLS), rows),
                      pl.BlockSpec((r, w), trow),
                      pl.BlockSpec((r, w), trow),
                      pl.BlockSpec((nc * N_HEADS, c, c), const3),
                      pl.BlockSpec((r, w), const),
                      pl.BlockSpec((r, w), const),
                      pl.BlockSpec((1, w), const),
                      pl.BlockSpec((1, w), const)],
            out_specs=[pl.BlockSpec((r, w), rows),
                       pl.BlockSpec((1, N_HEADS, HEAD_DIM, HEAD_DIM), st)],
            out_shape=[jax.ShapeDtypeStruct((n, w), F32),
                       jax.ShapeDtypeStruct((batch, N_HEADS, HEAD_DIM, HEAD_DIM), F32)],
            scratch_shapes=[pltpu.VMEM((N_HEADS, HEAD_DIM, HEAD_DIM), F32)],
            compiler_params=_cparams(("parallel", "arbitrary")),
            name="ret_prompt",
        )(x, cos, sin, f(np.tile(dmat, (nc, 1, 1))), f(np.tile(qdec, (nc, 1))), f(np.tile(kdec, (nc, 1))),
          f(gc), ng)
        return out, ns
    nseq = SAMPLE_SEQS
    r = nseq * c
    seg = np.arange(r) // c
    same = (seg[:, None] == seg[None, :])[None]
    dblk = np.where(same, np.tile(dmat, (1, nseq, nseq)), 0.0)
    rows = lambda i: (i, 0)
    const = lambda i: (0, 0)
    const3 = lambda i: (0, 0, 0)
    st = lambda i: (i, 0, 0, 0)
    sblk = (nseq, N_HEADS, HEAD_DIM, HEAD_DIM)
    out, ns = pl.pallas_call(
        functools.partial(_ret_sample_kernel, nseq=nseq, c=c),
        grid=(batch // nseq,),
        in_specs=[pl.BlockSpec((r, RET_COLS), rows),
                  pl.BlockSpec((r, w), const),
                  pl.BlockSpec((r, w), const),
                  pl.BlockSpec((N_HEADS, r, r), const3),
                  pl.BlockSpec((r, w), const),
                  pl.BlockSpec((r, w), const),
                  pl.BlockSpec((1, w), const),
                  pl.BlockSpec((1, w), const),
                  pl.BlockSpec(sblk, st)],
        out_specs=[pl.BlockSpec((r, w), rows), pl.BlockSpec(sblk, st)],
        out_shape=[jax.ShapeDtypeStruct((n, w), F32),
                   jax.ShapeDtypeStruct((batch, N_HEADS, HEAD_DIM, HEAD_DIM), F32)],
        compiler_params=_cparams(("parallel",)),
        name="ret_sample",
    )(x, jnp.tile(cos, (nseq, 1)), jnp.tile(sin, (nseq, 1)), f(dblk), f(np.tile(qdec, (nseq, 1))),
      f(np.tile(kdec, (nseq, 1))), f(gc), ng, state)
    return out, ns


def _unit_lower_inverse(a, eye, c):
    inv = eye - a
    x = a
    cover = 2
    while cover < c // 2:
        xm = x.astype(BF16)
        x = _bmm('bij,bjk->bik', xm, xm)
        inv = inv + _bmm('bij,bjk->bik', inv.astype(BF16), x.astype(BF16))
        cover *= 2
    resid = eye - inv - _bmm3('bij,bjk->bik', a, inv)
    return inv + _bmm('bij,bjk->bik', inv.astype(BF16), resid.astype(BF16))


def _gdn_gates(ab, alog_ref, dtb_ref, tri_l, tri_u):
    g_log = -jnp.exp(alog_ref[...]) * _softplus(ab + dtb_ref[...])
    beta = jax.nn.sigmoid(ab)
    hi = lax.Precision.HIGHEST
    g_col = _dot(tri_l.astype(F32), g_log, precision=hi)
    g_row = _dot_tn(g_log, tri_u.astype(F32), precision=hi)
    return g_col, g_row, beta


def _l2norm_heads(x, base):
    out = []
    for h in range(N_HEADS):
        xr = x[:, _head_slice(h, base)]
        out.append(xr * lax.rsqrt(jnp.sum(xr * xr, axis=-1, keepdims=True) + EPS))
    return out


def _gdn_prompt_kernel(x_ref, cw_ref, alog_ref, dtb_ref, ng_ref, out_ref, nconv_ref, ns_ref, ext, s_scr,
                       *, r, c):
    t = pl.program_id(1)
    last = pl.num_programs(1) - 1
    w = GDN_WIDTH
    hist = SUBLANES - (CONV_W - 1)
    nc = r // c

    @pl.when(t == 0)
    def _():
        ext[0:SUBLANES, :] = jnp.zeros((SUBLANES, 3 * w), F32)
        s_scr[...] = jnp.zeros(s_scr.shape, F32)

    qkv = _silu(_conv_unit(ext, x_ref[:, 0:3 * w], cw_ref, r))

    @pl.when(t == last)
    def _():
        nconv_ref[0] = ext[r + hist:r + SUBLANES, :]

    ext[0:SUBLANES, :] = ext[r:r + SUBLANES, :]
    tri_l, _, tri_u = _segment_masks(r, c)
    g_col, g_row, beta = _gdn_gates(x_ref[:, 4 * w:4 * w + LANES], alog_ref, dtb_ref, tri_l, tri_u)
    qn = [a * (HEAD_DIM ** -0.5) for a in _l2norm_heads(qkv, 0)]
    kn = _l2norm_heads(qkv, w)
    rows = lambda n: slice(n * c, (n + 1) * c)

    def stack(fn):
        return jnp.stack([fn(n, h) for n in range(nc) for h in range(N_HEADS)], axis=0)

    q = stack(lambda n, h: qn[h][rows(n)])
    k = stack(lambda n, h: kn[h][rows(n)])
    v = stack(lambda n, h: qkv[rows(n), _head_slice(h, 2 * w)])
    gi = stack(lambda n, h: g_col[rows(n), h:h + 1])
    gj = stack(lambda n, h: g_row[h:h + 1, rows(n)])
    bi = stack(lambda n, h: beta[rows(n), GDN_BETA_LANE + h:GDN_BETA_LANE + h + 1])
    gl = stack(lambda n, h: g_col[(n + 1) * c - 1:(n + 1) * c, h:h + 1])
    lower, strict, _ = _segment_masks(c, c)
    eye = (lower & jnp.logical_not(strict)).astype(F32)[None]
    decay = jnp.exp(jnp.where(lower[None], gi - gj, -jnp.inf))
    kb = k.astype(BF16)
    a_mat = jnp.where(strict[None], bi * _bmm('bid,bjd->bij', kb, kb) * decay, 0.0)
    tm = _unit_lower_inverse(a_mat, eye, c)
    eg = jnp.exp(gi)
    rhs = jnp.concatenate([bi * v, (bi * eg) * k], axis=2)
    uw = _bmm('bij,bjk->bik', tm.astype(BF16), rhs.astype(BF16))
    uu = uw[:, :, 0:HEAD_DIM]
    pm = (_bmm('bid,bjd->bij', q.astype(BF16), kb) * decay).astype(BF16)
    wq = jnp.concatenate([uw[:, :, HEAD_DIM:2 * HEAD_DIM], q * eg], axis=1).astype(BF16)
    kd = (k * jnp.exp(gl - gi)).astype(BF16)
    egl = jnp.exp(gl)
    z = x_ref[:, 3 * w:4 * w]
    s = s_scr[...]
    for n in range(nc):
        ps = slice(n * N_HEADS, (n + 1) * N_HEADS)
        wqs = _bmm('hid,hde->hie', wq[ps], s.astype(BF16))
        vn = (uu[ps] - wqs[:, 0:c]).astype(BF16)
        o = wqs[:, c:2 * c] + _bmm('hij,hje->hie', pm[ps], vn)
        s = egl[ps] * s + _bmm('hid,hie->hde', kd[ps], vn)
        on = _head_rms(o) * ng_ref[...]
        for h in range(N_HEADS):
            hs = _head_slice(h)
            out_ref[rows(n), hs] = on[h] * _silu(z[rows(n), hs])
    s_scr[...] = s

    @pl.when(t == last)
    def _():
        ns_ref[0] = s


def _gdn_sample_kernel(x_ref, cw_ref, alog_ref, dtb_ref, ng_ref, buf_ref, s0_ref,
                       out_ref, nconv_ref, ns_ref, ext, y_scr, *, nseq, c):
    w = GDN_WIDTH
    hist = SUBLANES - (CONV_W - 1)
    r = nseq * c

    def conv_body(u, carry_):
        r0 = pl.multiple_of(u * c, SUBLANES)
        ext[hist:SUBLANES, :] = buf_ref[u]
        y_scr[pl.ds(r0, c), :] = _conv_unit(ext, x_ref[pl.ds(r0, c), 0:3 * w], cw_ref, c)
        nconv_ref[u] = ext[c + hist:c + SUBLANES, :]
        return carry_

    lax.fori_loop(0, nseq, conv_body, 0)
    qkv = _silu(y_scr[...])
    lower, strict, upper = _segment_masks(r, c)
    g_col, g_row, beta = _gdn_gates(x_ref[:, 4 * w:4 * w + LANES], alog_ref, dtb_ref, lower, upper)
    g_seq = g_col.reshape(nseq, c, LANES)
    g_end = g_seq[:, c - 1:c, :]
    g_end_rows = jnp.broadcast_to(g_end, (nseq, c, LANES)).reshape(r, LANES)
    qn = [a * (HEAD_DIM ** -0.5) for a in _l2norm_heads(qkv, 0)]
    kn = _l2norm_heads(qkv, w)
    heads = range(N_HEADS)
    hstack = lambda fn: jnp.stack([fn(h) for h in heads], axis=0)
    q = hstack(lambda h: qn[h])
    k = hstack(lambda h: kn[h])
    v = hstack(lambda h: qkv[:, _head_slice(h, 2 * w)])
    gi = hstack(lambda h: g_col[:, h:h + 1])
    gj = hstack(lambda h: g_row[h:h + 1, :])
    bi = hstack(lambda h: beta[:, GDN_BETA_LANE + h:GDN_BETA_LANE + h + 1])
    ge = hstack(lambda h: g_end_rows[:, h:h + 1])
    eye = (lower & jnp.logical_not(strict)).astype(F32)[None]
    decay = jnp.exp(jnp.where(lower[None], gi - gj, -jnp.inf))
    kb = k.astype(BF16)
    a_mat = jnp.where(strict[None], bi * _bmm('bid,bjd->bij', kb, kb) * decay, 0.0)
    tm = _unit_lower_inverse(a_mat, eye, c)
    eg = jnp.exp(gi)
    rhs = jnp.concatenate([bi * v, (bi * eg) * k], axis=2)
    uw = _bmm('bij,bjk->bik', tm.astype(BF16), rhs.astype(BF16))
    pm = (_bmm('bid,bjd->bij', q.astype(BF16), kb) * decay).astype(BF16)
    qg = q * eg
    kd = k * jnp.exp(ge - gi)
    per_seq = lambda a: a.reshape(nseq, c, HEAD_DIM)
    s0 = [s0_ref[:, h] for h in heads]
    wqs = [_bmm('bid,bde->bie',
                jnp.concatenate([per_seq(uw[h][:, HEAD_DIM:2 * HEAD_DIM]), per_seq(qg[h])], axis=1), s0[h])
           for h in heads]
    vn = [per_seq(uw[h][:, 0:HEAD_DIM]) - wqs[h][:, 0:c] for h in heads]
    o = [wqs[h][:, c:2 * c].reshape(r, HEAD_DIM)
         + _dot(pm[h], vn[h].reshape(r, HEAD_DIM).astype(BF16)) for h in heads]
    upd = [_bmm('bid,bie->bde', per_seq(kd[h]), vn[h]) for h in heads]
    z = x_ref[:, 3 * w:4 * w]
    for h in heads:
        hs = _head_slice(h)
        ns_ref[:, h] = jnp.exp(g_end[:, :, h:h + 1]) * s0[h] + upd[h]
        out_ref[:, hs] = _head_rms(o[h]) * ng_ref[...] * _silu(z[:, hs])


def _gdn(x, cw, alog, dtb, ng, state, *, batch, seq):
    w = GDN_WIDTH
    n = batch * seq
    c = math.gcd(seq, CHUNK)
    if state is None:
        r = PROMPT_BLOCK
        nt = seq // r
        rows = lambda b, t: (b * nt + t, 0)
        const = lambda b, t: (0, 0)
        st3 = lambda b, t: (b, 0, 0)
        st4 = lambda b, t: (b, 0, 0, 0)
        return pl.pallas_call(
            functools.partial(_gdn_prompt_kernel, r=r, c=c),
            grid=(batch, nt),
            in_specs=[pl.BlockSpec((r, GDN_COLS), rows),
                      pl.BlockSpec((CONV_W, 3 * w), const),
                      pl.BlockSpec((1, LANES), const),
                      pl.BlockSpec((1, LANES), const),
                      pl.BlockSpec((1, HEAD_DIM), const)],
            out_specs=[pl.BlockSpec((r, w), rows),
                       pl.BlockSpec((1, CONV_W - 1, 3 * w), st3),
                       pl.BlockSpec((1, N_HEADS, HEAD_DIM, HEAD_DIM), st4)],
            out_shape=[jax.ShapeDtypeStruct((n, w), F32),
                       jax.ShapeDtypeStruct((batch, CONV_W - 1, 3 * w), F32),
                       jax.ShapeDtypeStruct((batch, N_HEADS, HEAD_DIM, HEAD_DIM), F32)],
            scratch_shapes=[pltpu.VMEM((r + SUBLANES, 3 * w), F32),
                            pltpu.VMEM((N_HEADS, HEAD_DIM, HEAD_DIM), F32)],
            compiler_params=_cparams(("parallel", "arbitrary")),
            name="gdn_prompt",
        )(x, cw, alog, dtb, ng)
    nseq = SAMPLE_SEQS
    r = nseq * c
    rows = lambda i: (i, 0)
    const = lambda i: (0, 0)
    st3 = lambda i: (i, 0, 0)
    st4 = lambda i: (i, 0, 0, 0)
    cblk = (nseq, CONV_W - 1, 3 * w)
    sblk = (nseq, N_HEADS, HEAD_DIM, HEAD_DIM)
    return pl.pallas_call(
        functools.partial(_gdn_sample_kernel, nseq=nseq, c=c),
        grid=(batch // nseq,),
        in_specs=[pl.BlockSpec((r, GDN_COLS), rows),
                  pl.BlockSpec((CONV_W, 3 * w), const),
                  pl.BlockSpec((1, LANES), const),
                  pl.BlockSpec((1, LANES), const),
                  pl.BlockSpec((1, HEAD_DIM), const),
                  pl.BlockSpec(cblk, st3),
                  pl.BlockSpec(sblk, st4)],
        out_specs=[pl.BlockSpec((r, w), rows), pl.BlockSpec(cblk, st3), pl.BlockSpec(sblk, st4)],
        out_shape=[jax.ShapeDtypeStruct((n, w), F32),
                   jax.ShapeDtypeStruct((batch, CONV_W - 1, 3 * w), F32),
                   jax.ShapeDtypeStruct((batch, N_HEADS, HEAD_DIM, HEAD_DIM), F32)],
        scratch_shapes=[pltpu.VMEM((c + SUBLANES, 3 * w), F32), pltpu.VMEM((r, 3 * w), F32)],
        compiler_params=_cparams(("parallel",)),
        name="gdn_sample",
    )(x, cw, alog, dtb, ng, *state)


def _block_diag(wb):
    out = jnp.zeros((LRU_WIDTH, LRU_WIDTH), wb.dtype)
    for n in range(LRU_BLOCKS):
        s = slice(n * LRU_BLOCK, (n + 1) * LRU_BLOCK)
        out = out.at[s, s].set(wb[n])
    return out


def _pad_lanes(v, offset=0):
    return jnp.zeros((1, LANES), F32).at[0, offset:offset + v.shape[0]].set(v.astype(F32))


def _prep_layer(W, l):
    row = lambda v: v.reshape(1, -1).astype(F32)
    pad = IN_COLS - W['w_in'].shape[2]
    wr = jnp.zeros((D_MODEL, LANES), F32)
    wr = wr.at[:, ROUTER_GROUP_LANE:ROUTER_GROUP_LANE + N_GROUPS].set(W['w_router_group'][l])
    wr = wr.at[:, ROUTER_EXPERT_LANE:ROUTER_EXPERT_LANE + N_EXPERTS].set(W['w_router_expert'][l])
    br = jnp.zeros((1, LANES), F32)
    br = br.at[0, ROUTER_GROUP_LANE:ROUTER_GROUP_LANE + N_GROUPS].set(W['b_router_group'][l])
    br = br.at[0, ROUTER_EXPERT_LANE:ROUTER_EXPERT_LANE + N_EXPERTS].set(W['b_router_expert'][l])
    return dict(
        norm_mix_g=row(W['norm_mix_g'][l]),
        w_in=jnp.pad(W['w_in'][l], ((0, 0), (0, pad))).astype(BF16),
        lru_conv_w=W['lru_conv_w'][l], lru_conv_b=row(W['lru_conv_b'][l]),
        lru_wa=_block_diag(W['lru_wa'][l]).astype(BF16), lru_ba=row(W['lru_ba'][l]),
        lru_wx=_block_diag(W['lru_wx'][l]).astype(BF16), lru_bx=row(W['lru_bx'][l]),
        lru_lambda=row(W['lru_lambda'][l]),
        ret_norm_g=row(W['ret_norm_g'][l]),
        gdn_conv_w=W['gdn_conv_w'][l],
        gdn_a_log=_pad_lanes(W['gdn_a_log'][l]), gdn_dt_bias=_pad_lanes(W['gdn_dt_bias'][l]),
        gdn_norm_g=row(W['gdn_norm_g'][l]),
        w_out=W['w_out'][l].astype(BF16),
        norm_ffn_g=row(W['norm_ffn_g'][l]),
        w_router=wr.astype(BF16), b_router=br,
        w_expert_gate=W['w_expert_gate'][l].astype(BF16),
        w_expert_up=W['w_expert_up'][l].astype(BF16),
        w_expert_down=W['w_expert_down'][l].astype(BF16),
        norm_pe_g=row(W['norm_pe_g'][l]),
        w_pe=W['w_pe'][l].astype(BF16),
        w_pe_gate=W['w_pe_gate'][l].astype(BF16),
    )


def _trunk(x, p, states, pos0, layers, final_g):
    batch, seq, _ = x.shape
    n = batch * seq
    h = x.reshape(n, D_MODEL)
    new = ([], [], [], [], [])
    for l, L in enumerate(layers):
        if states is None:
            st_lru = st_ret = st_gdn = None
        else:
            st_lru = (states[0][l], states[1][l])
            st_ret = states[2][l]
            st_gdn = (states[3][l], states[4][l])
        lru_x, ret_x, gdn_x = _in_proj(h, L['norm_mix_g'], L['w_in'])
        out_a, n_lru_conv, n_lru_h = _lru(
            lru_x, L['lru_conv_w'], L['lru_conv_b'], L['lru_wa'], L['lru_ba'], L['lru_wx'],
            L['lru_bx'], L['lru_lambda'], st_lru, batch=batch, seq=seq)
        out_b, n_ret = _ret(ret_x, L['ret_norm_g'], st_ret, batch=batch, seq=seq, pos0=pos0)
        out_c, n_gdn_conv, n_gdn = _gdn(
            gdn_x, L['gdn_conv_w'], L['gdn_a_log'], L['gdn_dt_bias'], L['gdn_norm_g'], st_gdn,
            batch=batch, seq=seq)
        h1, xn, gates = _out_router(h, out_a, out_b, out_c, L['w_out'], L['norm_ffn_g'],
                                    L['w_router'], L['b_router'])
        h2 = _moe(xn, gates, h1, L['w_expert_gate'], L['w_expert_up'], L['w_expert_down'])
        h = _pe(h2, p[l].reshape(n, PE_DIM), L['norm_pe_g'], L['w_pe_gate'], L['w_pe'], final_g,
                final=(l == len(layers) - 1))
        for lst, s in zip(new, (n_lru_conv, n_lru_h, n_ret, n_gdn_conv, n_gdn)):
            lst.append(s)
    return h.reshape(batch, seq, D_MODEL), tuple(jnp.stack(lst) for lst in new)


def kernel(x_prompt, x_sample, p_prompt, p_sample, state_lru_conv, state_lru_h, state_ret, state_gdn_conv, state_gdn, norm_mix_g, w_in, lru_conv_w, lru_conv_b, lru_wa, lru_ba, lru_wx, lru_bx, lru_lambda, ret_norm_g, gdn_conv_w, gdn_a_log, gdn_dt_bias, gdn_norm_g, w_out, norm_ffn_g, w_router_group, b_router_group, w_router_expert, b_router_expert, w_expert_gate, w_expert_up, w_expert_down, norm_pe_g, w_pe, w_pe_gate, final_norm_g):
    W = dict(norm_mix_g=norm_mix_g, w_in=w_in, lru_conv_w=lru_conv_w, lru_conv_b=lru_conv_b,
             lru_wa=lru_wa, lru_ba=lru_ba, lru_wx=lru_wx, lru_bx=lru_bx, lru_lambda=lru_lambda,
             ret_norm_g=ret_norm_g, gdn_conv_w=gdn_conv_w, gdn_a_log=gdn_a_log, gdn_dt_bias=gdn_dt_bias,
             gdn_norm_g=gdn_norm_g, w_out=w_out, norm_ffn_g=norm_ffn_g, w_router_group=w_router_group,
             b_router_group=b_router_group, w_router_expert=w_router_expert, b_router_expert=b_router_expert,
             w_expert_gate=w_expert_gate, w_expert_up=w_expert_up, w_expert_down=w_expert_down,
             norm_pe_g=norm_pe_g, w_pe=w_pe, w_pe_gate=w_pe_gate)
    layers = [_prep_layer(W, l) for l in range(DEPTH)]
    final_g = final_norm_g.reshape(1, D_MODEL).astype(F32)
    y_p, st_p = _trunk(x_prompt, p_prompt, None, 0, layers, final_g)
    sample_states = (state_lru_conv, state_lru_h, state_ret, state_gdn_conv, state_gdn)
    y_s, st_s = _trunk(x_sample, p_sample, sample_states, PAST_LEN, layers, final_g)
    return (y_p, y_s) + st_p + st_s
```

```python
import functools
import math

import jax
import jax.numpy as jnp
import numpy as np
from jax import lax
from jax.experimental import pallas as pl
from jax.experimental.pallas import tpu as pltpu

F32 = jnp.float32
BF16 = jnp.bfloat16

D_MODEL = 1024
DEPTH = 2
PAST_LEN = 16384
HEAD_DIM = 64
HALF = HEAD_DIM // 2
LRU_WIDTH = 256
LRU_BLOCKS = 4
LRU_BLOCK = 64
LRU_C = 8.0
RET_WIDTH = 384
GDN_WIDTH = 384
N_HEADS = 6
CONV_W = 4
CHUNK = 64
ROPE_BASE = 10000.0
PE_DIM = 256
N_GROUPS = 4
EXPERTS_PER_GROUP = 4
N_EXPERTS = 16
D_EXPERT = 256
EPS = 1e-6

LANES = 128
SUBLANES = 8
VMEM_LIMIT = 48 * 1024 * 1024

LRU_COLS = 2 * LRU_WIDTH
RET_COLS = 4 * RET_WIDTH
GDN_COLS = 4 * GDN_WIDTH + LANES
IN_COLS = LRU_COLS + RET_COLS + GDN_COLS
ROUTER_GROUP_LANE = 0
ROUTER_EXPERT_LANE = N_GROUPS
GDN_BETA_LANE = N_HEADS
GSEL_LANE = 0
D_GROUP = EXPERTS_PER_GROUP * D_EXPERT

TOKEN_TILE = 256
MOE_TILE = 512
PROMPT_BLOCK = 256
SAMPLE_SEQS = 16


def _cparams(sem):
    return pltpu.CompilerParams(dimension_semantics=sem, vmem_limit_bytes=VMEM_LIMIT)


def _rmsnorm(x, g):
    return x * lax.rsqrt(jnp.mean(x * x, axis=-1, keepdims=True) + EPS) * g


def _dot(a, b, **kw):
    return jnp.dot(a, b, preferred_element_type=F32, **kw)


def _dot_nt(a, b):
    return lax.dot_general(a, b, (((1,), (1,)), ((), ())), preferred_element_type=F32)


def _dot_tn(a, b, **kw):
    return lax.dot_general(a, b, (((0,), (0,)), ((), ())), preferred_element_type=F32, **kw)


def _bmm(spec, a, b):
    return jnp.einsum(spec, a, b, preferred_element_type=F32)


def _bmm3(spec, a, b):
    a_hi = a.astype(BF16)
    a_lo = (a - a_hi.astype(F32)).astype(BF16)
    b_hi = b.astype(BF16)
    b_lo = (b - b_hi.astype(F32)).astype(BF16)
    return _bmm(spec, a_hi, b_hi) + (_bmm(spec, a_hi, b_lo) + _bmm(spec, a_lo, b_hi))


def _softplus(x):
    return jnp.maximum(x, 0.0) + jnp.log1p(jnp.exp(-jnp.abs(x)))


def _silu(x):
    return x * jax.nn.sigmoid(x)


def _gelu_tanh(x):
    return 0.5 * x * (1.0 + jnp.tanh(math.sqrt(2.0 / math.pi) * (x + 0.044715 * (x * x * x))))


def _in_proj_kernel(h_ref, g_ref, w_ref, lru_ref, ret_ref, gdn_ref):
    xn = _rmsnorm(h_ref[...], g_ref[...]).astype(BF16)
    lru_ref[...] = _dot(xn, w_ref[:, 0:LRU_COLS])
    ret_ref[...] = _dot(xn, w_ref[:, LRU_COLS:LRU_COLS + RET_COLS])
    gdn_ref[...] = _dot(xn, w_ref[:, LRU_COLS + RET_COLS:IN_COLS])


def _in_proj(h, g, w):
    n = h.shape[0]
    tm = TOKEN_TILE
    row = lambda i: (i, 0)
    const = lambda i: (0, 0)
    return pl.pallas_call(
        _in_proj_kernel,
        grid=(n // tm,),
        in_specs=[pl.BlockSpec((tm, D_MODEL), row),
                  pl.BlockSpec((1, D_MODEL), const),
                  pl.BlockSpec((D_MODEL, IN_COLS), const)],
        out_specs=[pl.BlockSpec((tm, LRU_COLS), row),
                   pl.BlockSpec((tm, RET_COLS), row),
                   pl.BlockSpec((tm, GDN_COLS), row)],
        out_shape=[jax.ShapeDtypeStruct((n, LRU_COLS), F32),
                   jax.ShapeDtypeStruct((n, RET_COLS), F32),
                   jax.ShapeDtypeStruct((n, GDN_COLS), F32)],
        compiler_params=_cparams(("parallel",)),
        name="in_proj",
    )(h, g, w)


def _route(logits):
    lane = lax.broadcasted_iota(jnp.int32, logits.shape, 1).astype(F32)
    neg = -jnp.inf
    far = float(LANES)
    gmask = lane < N_GROUPS
    gl = jnp.where(gmask, logits, neg)
    gmax = jnp.max(gl, axis=-1, keepdims=True)
    gsel = jnp.min(jnp.where(gl == gmax, lane, far), axis=-1, keepdims=True)
    gsum = jnp.sum(jnp.where(gmask, jnp.exp(logits - gmax), 0.0), axis=-1, keepdims=True)
    g_w = 1.0 / gsum
    lo = ROUTER_EXPERT_LANE + EXPERTS_PER_GROUP * gsel
    el = jnp.where(lane >= lo, jnp.where(lane < lo + EXPERTS_PER_GROUP, logits, neg), neg)
    m1 = jnp.max(el, axis=-1, keepdims=True)
    i1 = jnp.min(jnp.where(el == m1, lane, far), axis=-1, keepdims=True)
    el2 = jnp.where(lane == i1, neg, el)
    m2 = jnp.max(el2, axis=-1, keepdims=True)
    i2 = jnp.min(jnp.where(el2 == m2, lane, far), axis=-1, keepdims=True)
    e2 = jnp.exp(m2 - m1)
    w1 = g_w / (1.0 + e2)
    w2 = w1 * e2
    gates = jnp.where(lane == i1, w1, 0.0) + jnp.where(lane == i2, w2, 0.0)
    return gates + jnp.where(lane == GSEL_LANE, gsel, 0.0)


def _out_router_kernel(h_ref, a_ref, b_ref, c_ref, wo_ref, g_ref, wr_ref, br_ref,
                       h1_ref, xn_ref, gates_ref):
    o1 = LRU_WIDTH
    o2 = LRU_WIDTH + RET_WIDTH
    mix = (_dot(a_ref[...].astype(BF16), wo_ref[0:o1, :])
           + _dot(b_ref[...].astype(BF16), wo_ref[o1:o2, :])
           + _dot(c_ref[...].astype(BF16), wo_ref[o2:D_MODEL, :]))
    h1 = h_ref[...] + mix
    h1_ref[...] = h1
    xn = _rmsnorm(h1, g_ref[...]).astype(BF16)
    xn_ref[...] = xn
    gates_ref[...] = _route(_dot(xn, wr_ref[...]) + br_ref[...])


def _out_router(h, a, b, c, wo, g, wr, br):
    n = h.shape[0]
    tm = TOKEN_TILE
    row = lambda i: (i, 0)
    const = lambda i: (0, 0)
    return pl.pallas_call(
        _out_router_kernel,
        grid=(n // tm,),
        in_specs=[pl.BlockSpec((tm, D_MODEL), row),
                  pl.BlockSpec((tm, LRU_WIDTH), row),
                  pl.BlockSpec((tm, RET_WIDTH), row),
                  pl.BlockSpec((tm, GDN_WIDTH), row),
                  pl.BlockSpec((D_MODEL, D_MODEL), const),
                  pl.BlockSpec((1, D_MODEL), const),
                  pl.BlockSpec((D_MODEL, LANES), const),
                  pl.BlockSpec((1, LANES), const)],
        out_specs=[pl.BlockSpec((tm, D_MODEL), row),
                   pl.BlockSpec((tm, D_MODEL), row),
                   pl.BlockSpec((tm, LANES), row)],
        out_shape=[jax.ShapeDtypeStruct((n, D_MODEL), F32),
                   jax.ShapeDtypeStruct((n, D_MODEL), BF16),
                   jax.ShapeDtypeStruct((n, LANES), F32)],
        compiler_params=_cparams(("parallel",)),
        name="out_router",
    )(h, a, b, c, wo, g, wr, br)


def _moe_kernel(tg_ref, tv_ref, xn_ref, gates_ref, wg_ref, wu_ref, wd_ref, out_ref):
    i = pl.program_id(0)

    @pl.when(tv_ref[i] == 0)
    def _():
        out_ref[...] = jnp.zeros(out_ref.shape, F32)

    @pl.when(tv_ref[i] != 0)
    def _():
        xn = xn_ref[...]
        hg = _dot(xn, wg_ref[...])
        hu = _dot(xn, wu_ref[...])
        gates = gates_ref[...]
        lane = lax.broadcasted_iota(jnp.int32, gates.shape, 1)
        base = ROUTER_EXPERT_LANE + EXPERTS_PER_GROUP * tg_ref[i]
        parts = []
        for j in range(EXPERTS_PER_GROUP):
            sl = slice(j * D_EXPERT, (j + 1) * D_EXPERT)
            gcol = jnp.sum(jnp.where(lane == base + j, gates, 0.0), axis=-1, keepdims=True)
            parts.append((_silu(hg[:, sl]) * hu[:, sl] * gcol).astype(BF16))
        out_ref[...] = _dot(jnp.concatenate(parts, axis=1), wd_ref[...])


def _moe(xn, gates, wg, wu, wd):
    n = xn.shape[0]
    tm = MOE_TILE
    npad = n + N_GROUPS * tm
    ntiles = npad // tm
    i32 = jnp.int32
    grp = gates[:, GSEL_LANE].astype(i32)
    counts = jnp.sum((grp[:, None] == jnp.arange(N_GROUPS, dtype=i32)[None]).astype(i32), axis=0)
    padded = ((counts + tm - 1) // tm) * tm
    pend = jnp.cumsum(padded)
    pstart = pend - padded
    ustart = jnp.cumsum(counts) - counts
    order = jnp.argsort(grp, stable=True).astype(i32)
    gs = grp[order]
    dest = pstart[gs] + (jnp.arange(n, dtype=i32) - ustart[gs])
    src = jnp.zeros((npad,), i32).at[dest].set(order)
    valid = jnp.zeros((npad,), F32).at[dest].set(1.0)
    slot = jnp.zeros((n,), i32).at[order].set(dest)
    tile_start = jnp.arange(ntiles, dtype=i32) * tm
    tg = jnp.minimum(jnp.sum((tile_start[:, None] >= pend[None, :]).astype(i32), axis=1), N_GROUPS - 1)
    tv = (tile_start < pend[-1]).astype(i32)
    row = lambda i, tg_, tv_: (i, 0)
    wsel = lambda i, tg_, tv_: (tg_[i], 0, 0)
    ys = pl.pallas_call(
        _moe_kernel,
        grid_spec=pltpu.PrefetchScalarGridSpec(
            num_scalar_prefetch=2,
            grid=(ntiles,),
            in_specs=[pl.BlockSpec((tm, D_MODEL), row),
                      pl.BlockSpec((tm, LANES), row),
                      pl.BlockSpec((None, D_MODEL, D_GROUP), wsel),
                      pl.BlockSpec((None, D_MODEL, D_GROUP), wsel),
                      pl.BlockSpec((None, D_GROUP, D_MODEL), wsel)],
            out_specs=pl.BlockSpec((tm, D_MODEL), row)),
        out_shape=jax.ShapeDtypeStruct((npad, D_MODEL), F32),
        compiler_params=_cparams(("arbitrary",)),
        name="moe",
    )(tg, tv, xn[src], gates[src] * valid[:, None], wg, wu, wd)
    return ys[slot]


def _pe_kernel(h_ref, y_ref, p_ref, g_ref, wgate_ref, wpe_ref, fg_ref, out_ref, *, final):
    h = h_ref[...] + y_ref[...]
    xn = _rmsnorm(h, g_ref[...]).astype(BF16)
    gate = jax.nn.sigmoid(_dot(xn, wgate_ref[...]))
    pe = _dot(p_ref[...].astype(BF16), wpe_ref[...])
    h3 = h + pe * gate
    out_ref[...] = _rmsnorm(h3, fg_ref[...]) if final else h3


def _pe(h, y, p, g, wgate, wpe, fg, final):
    n = h.shape[0]
    tm = TOKEN_TILE
    row = lambda i: (i, 0)
    const = lambda i: (0, 0)
    return pl.pallas_call(
        functools.partial(_pe_kernel, final=final),
        grid=(n // tm,),
        in_specs=[pl.BlockSpec((tm, D_MODEL), row),
                  pl.BlockSpec((tm, D_MODEL), row),
                  pl.BlockSpec((tm, PE_DIM), row),
                  pl.BlockSpec((1, D_MODEL), const),
                  pl.BlockSpec((D_MODEL, D_MODEL), const),
                  pl.BlockSpec((PE_DIM, D_MODEL), const),
                  pl.BlockSpec((1, D_MODEL), const)],
        out_specs=pl.BlockSpec((tm, D_MODEL), row),
        out_shape=jax.ShapeDtypeStruct((n, D_MODEL), F32),
        compiler_params=_cparams(("parallel",)),
        name="pe_gate",
    )(h, y, p, g, wgate, wpe, fg)


def _conv_unit(ext, x, w_ref, c):
    ext[SUBLANES:SUBLANES + c, :] = x
    y = w_ref[CONV_W - 1:CONV_W, :] * x
    for j in range(CONV_W - 1):
        off = SUBLANES - (CONV_W - 1) + j
        y = y + w_ref[j:j + 1, :] * ext[off:off + c, :]
    return y


def _lru_kernel(*refs, c, units, carry):
    if carry:
        (x_ref, cw_ref, cb_ref, wa_ref, ba_ref, wx_ref, bx_ref, lam_ref,
         out_ref, nconv_ref, nh_ref, ext, y_scr, a_scr, b_scr, hcar) = refs
    else:
        (x_ref, cw_ref, cb_ref, wa_ref, ba_ref, wx_ref, bx_ref, lam_ref, buf_ref, h0_ref,
         out_ref, nconv_ref, nh_ref, ext, y_scr, a_scr, b_scr) = refs
    w = LRU_WIDTH
    hist = SUBLANES - (CONV_W - 1)
    if carry:
        t = pl.program_id(1)
        last = pl.num_programs(1) - 1

        @pl.when(t == 0)
        def _():
            ext[0:SUBLANES, :] = jnp.zeros((SUBLANES, w), F32)
            hcar[...] = jnp.zeros((1, w), F32)

    def conv_body(u, carry_):
        r0 = pl.multiple_of(u * c, SUBLANES)
        if not carry:
            ext[hist:SUBLANES, :] = buf_ref[u]
        x = x_ref[pl.ds(r0, c), 0:w]
        y_scr[pl.ds(r0, c), :] = _conv_unit(ext, x, cw_ref, c) + cb_ref[...]
        if carry:
            @pl.when(t == last)
            def _():
                nconv_ref[0] = ext[c + hist:c + SUBLANES, :]
            ext[0:SUBLANES, :] = ext[c:c + SUBLANES, :]
        else:
            nconv_ref[u] = ext[c + hist:c + SUBLANES, :]
        return carry_

    lax.fori_loop(0, units, conv_body, 0)

    y = y_scr[...]
    yb = y.astype(BF16)
    r = jax.nn.sigmoid(_dot(yb, wa_ref[...]) + ba_ref[...])
    i = jax.nn.sigmoid(_dot(yb, wx_ref[...]) + bx_ref[...])
    log_a = (-LRU_C) * r * _softplus(-lam_ref[...])
    th = jnp.tanh(log_a)
    a_scr[...] = jnp.exp(log_a)
    b_scr[...] = jnp.sqrt(-2.0 * th / (1.0 - th)) * (i * y)

    def scan_body(u, carry_):
        r0 = pl.multiple_of(u * c, SUBLANES)
        a = a_scr[pl.ds(r0, c), :]
        b = b_scr[pl.ds(r0, c), :]
        rowi = lax.broadcasted_iota(jnp.int32, (c, w), 0)
        d = 1
        while d < c:
            keep = rowi >= d
            a_s = pltpu.roll(a, d, axis=0)
            b_s = pltpu.roll(b, d, axis=0)
            b = jnp.where(keep, a * b_s + b, b)
            a = jnp.where(keep, a * a_s, a)
            d *= 2
        h0 = hcar[...] if carry else h0_ref[u]
        h = a * h0 + b
        gate = x_ref[pl.ds(r0, c), w:2 * w]
        out_ref[pl.ds(r0, c), :] = h * _gelu_tanh(gate)
        if carry:
            hcar[...] = h[c - 1:c, :]
        else:
            nh_ref[u] = h[c - 1:c, :]
        return carry_

    lax.fori_loop(0, units, scan_body, 0)
    if carry:
        @pl.when(t == last)
        def _():
            nh_ref[0] = hcar[...]


def _lru(x, cw, cb, wa, ba, wx, bx, lam, state, *, batch, seq):
    w = LRU_WIDTH
    carry = state is None
    if carry:
        c, units = PROMPT_BLOCK, 1
        nt = seq // c
        grid = (batch, nt)
        rows = lambda b, t: (b * nt + t, 0)
        const = lambda b, t: (0, 0)
        st = lambda b, t: (b, 0, 0)
        sem = ("parallel", "arbitrary")
        state_in, state_specs = [], []
    else:
        c, units = seq, batch
        grid = (1,)
        rows = lambda i: (0, 0)
        const = lambda i: (0, 0)
        st = lambda i: (0, 0, 0)
        sem = ("arbitrary",)
        state_in = [state[0], state[1].reshape(batch, 1, w)]
        state_specs = [pl.BlockSpec((units, CONV_W - 1, w), st), pl.BlockSpec((units, 1, w), st)]
    n = batch * seq
    blk = c * units
    scratch = [pltpu.VMEM((c + SUBLANES, w), F32), pltpu.VMEM((blk, w), F32),
               pltpu.VMEM((blk, w), F32), pltpu.VMEM((blk, w), F32)]
    if carry:
        scratch.append(pltpu.VMEM((1, w), F32))
    out, nconv, nh = pl.pallas_call(
        functools.partial(_lru_kernel, c=c, units=units, carry=carry),
        grid=grid,
        in_specs=[pl.BlockSpec((blk, LRU_COLS), rows),
                  pl.BlockSpec((CONV_W, w), const),
                  pl.BlockSpec((1, w), const),
                  pl.BlockSpec((w, w), const),
                  pl.BlockSpec((1, w), const),
                  pl.BlockSpec((w, w), const),
                  pl.BlockSpec((1, w), const),
                  pl.BlockSpec((1, w), const)] + state_specs,
        out_specs=[pl.BlockSpec((blk, w), rows),
                   pl.BlockSpec((units, CONV_W - 1, w), st),
                   pl.BlockSpec((units, 1, w), st)],
        out_shape=[jax.ShapeDtypeStruct((n, w), F32),
                   jax.ShapeDtypeStruct((batch, CONV_W - 1, w), F32),
                   jax.ShapeDtypeStruct((batch, 1, w), F32)],
        scratch_shapes=scratch,
        compiler_params=_cparams(sem),
        name="lru_prompt" if carry else "lru_sample",
    )(x, cw, cb, wa, ba, wx, bx, lam, *state_in)
    return out, nconv, nh.reshape(batch, w)


def _rope(x, cos, sin):
    parts = []
    for j in range(x.shape[1] // LANES):
        sl = slice(j * LANES, (j + 1) * LANES)
        xs = x[:, sl]
        lane = lax.broadcasted_iota(jnp.int32, xs.shape, 1)
        swapped = jnp.where((lane & HALF) == 0,
                            pltpu.roll(xs, LANES - HALF, axis=1), pltpu.roll(xs, HALF, axis=1))
        parts.append(xs * cos[:, sl] + swapped * sin[:, sl])
    return jnp.concatenate(parts, axis=1)


def _segment_masks(r, c):
    ri = lax.broadcasted_iota(jnp.int32, (r, r), 0)
    ci = lax.broadcasted_iota(jnp.int32, (r, r), 1)
    if c == r:
        return ri >= ci, ri > ci, ri <= ci
    shift = c.bit_length() - 1
    same = lax.shift_right_logical(ri, shift) == lax.shift_right_logical(ci, shift)
    return same & (ri >= ci), same & (ri > ci), same & (ri <= ci)


def _head_slice(h, base=0):
    return slice(base + h * HEAD_DIM, base + (h + 1) * HEAD_DIM)


def _head_rms(o):
    return o * lax.rsqrt(jnp.mean(o * o, axis=-1, keepdims=True) + EPS)


def _ret_prompt_kernel(x_ref, cos_ref, sin_ref, dmat_ref, qdec_ref, kdec_ref, gc_ref, ng_ref,
                       out_ref, ns_ref, s_scr, *, r, c):
    t = pl.program_id(1)
    last = pl.num_programs(1) - 1

    @pl.when(t == 0)
    def _():
        s_scr[...] = jnp.zeros(s_scr.shape, F32)

    w = RET_WIDTH
    nc = r // c
    cos = cos_ref[...]
    sin = sin_ref[...]
    q = _rope(x_ref[:, 0:w], cos, sin) * (HEAD_DIM ** -0.5)
    k = _rope(x_ref[:, w:2 * w], cos, sin)
    v = x_ref[:, 2 * w:3 * w]
    qd = q * qdec_ref[...]
    kd = k * kdec_ref[...]
    rows = lambda n: slice(n * c, (n + 1) * c)

    def stack(a):
        return jnp.stack([a[rows(n), _head_slice(h)] for n in range(nc) for h in range(N_HEADS)], axis=0)

    qs, ks, vs = stack(q).astype(BF16), stack(k).astype(BF16), stack(v).astype(BF16)
    scores = _bmm('bid,bjd->bij', qs, ks) * dmat_ref[...]
    o_inner = _bmm('bij,bje->bie', scores.astype(BF16), vs)
    kv = _bmm('bid,bie->bde', stack(kd).astype(BF16), vs)
    gc = jnp.stack([gc_ref[:, _head_slice(h)] for h in range(N_HEADS)], axis=0)
    s = s_scr[...]
    starts = []
    for n in range(nc):
        starts.append(s)
        s = gc * s + kv[n * N_HEADS:(n + 1) * N_HEADS]
    s_scr[...] = s

    @pl.when(t == last)
    def _():
        ns_ref[0] = s

    o = o_inner + _bmm('bid,bde->bie', stack(qd).astype(BF16), jnp.concatenate(starts, axis=0).astype(BF16))
    on = _head_rms(o)
    g = x_ref[:, 3 * w:4 * w]
    for n in range(nc):
        for h in range(N_HEADS):
            hs = _head_slice(h)
            out_ref[rows(n), hs] = on[n * N_HEADS + h] * ng_ref[:, hs] * _silu(g[rows(n), hs])


def _ret_sample_kernel(x_ref, cos_ref, sin_ref, dmat_ref, qdec_ref, kdec_ref, gc_ref, ng_ref, s0_ref,
                       out_ref, ns_ref, *, nseq, c):
    w = RET_WIDTH
    cos = cos_ref[...]
    sin = sin_ref[...]
    q = _rope(x_ref[:, 0:w], cos, sin) * (HEAD_DIM ** -0.5)
    k = _rope(x_ref[:, w:2 * w], cos, sin)
    v = x_ref[:, 2 * w:3 * w]
    qd = q * qdec_ref[...]
    kd = k * kdec_ref[...]
    g = x_ref[:, 3 * w:4 * w]
    per_seq = lambda a: a.reshape(nseq, c, HEAD_DIM)
    heads = range(N_HEADS)
    vb = [v[:, _head_slice(h)].astype(BF16) for h in heads]
    scores = [_dot_nt(q[:, _head_slice(h)].astype(BF16), k[:, _head_slice(h)].astype(BF16)) * dmat_ref[h]
              for h in heads]
    o_inner = [_dot(scores[h].astype(BF16), vb[h]) for h in heads]
    s0 = [s0_ref[:, h] for h in heads]
    o_cross = [_bmm('bqd,bde->bqe', per_seq(qd[:, _head_slice(h)]), s0[h]) for h in heads]
    kv = [_bmm('bkd,bke->bde', per_seq(kd[:, _head_slice(h)]), per_seq(v[:, _head_slice(h)])) for h in heads]
    for h in heads:
        hs = _head_slice(h)
        ns_ref[:, h] = gc_ref[:, hs] * s0[h] + kv[h]
        o = o_inner[h] + o_cross[h].reshape(nseq * c, HEAD_DIM)
        out_ref[:, hs] = _head_rms(o) * ng_ref[:, hs] * _silu(g[:, hs])


def _rope_tables(pos0, seq):
    inv = np.exp(-math.log(ROPE_BASE) * np.arange(HALF, dtype=np.float64) / HALF)
    ang = (pos0 + np.arange(seq, dtype=np.float64))[:, None] * inv[None]
    cos = np.tile(np.concatenate([np.cos(ang), np.cos(ang)], axis=1), (1, N_HEADS))
    sin = np.tile(np.concatenate([-np.sin(ang), np.sin(ang)], axis=1), (1, N_HEADS))
    return jnp.asarray(cos, dtype=F32), jnp.asarray(sin, dtype=F32)


def _ret_decay_tables(c):
    hh = np.arange(N_HEADS, dtype=np.float64)
    lg = np.log(1.0 - 2.0 ** (-5.0 - hh))
    n = np.arange(c, dtype=np.float64)
    rel = n[:, None] - n[None, :]
    dmat = np.where(rel[None] >= 0, np.exp(np.maximum(rel, 0.0)[None] * lg[:, None, None]), 0.0)
    rep = lambda a: np.repeat(a, HEAD_DIM, axis=-1)
    qdec = rep(np.exp((n + 1.0)[:, None] * lg[None, :]))
    kdec = rep(np.exp((c - 1.0 - n)[:, None] * lg[None, :]))
    gc = rep(np.exp(c * lg)[None, :])
    return dmat, qdec, kdec, gc


def _ret(x, ng, state, *, batch, seq, pos0):
    w = RET_WIDTH
    n = batch * seq
    c = math.gcd(seq, CHUNK)
    dmat, qdec, kdec, gc = _ret_decay_tables(c)
    cos, sin = _rope_tables(pos0, seq)
    f = lambda a: jnp.asarray(a, dtype=F32)
    if state is None:
        r = PROMPT_BLOCK
        nc = r // c
        nt = seq // r
        rows = lambda b, t: (b * nt + t, 0)
        trow = lambda b, t: (t, 0)
        const = lambda b, t: (0, 0)
        const3 = lambda b, t: (0, 0, 0)
        st = lambda b, t: (b, 0, 0, 0)
        out, ns = pl.pallas_call(
            functools.partial(_ret_prompt_kernel, r=r, c=c),
            grid=(batch, nt),
            in_specs=[pl.BlockSpec((r, RET_COLS), rows),
                      pl.BlockSpec((r, w), trow),
                      pl.BlockSpec((r, w), trow),
                      pl.BlockSpec((nc * N_HEADS, c, c), const3),
                      pl.BlockSpec((r, w), const),
                      pl.BlockSpec((r, w), const),
                      pl.BlockSpec((1, w), const),
                      pl.BlockSpec((1, w), const)],
            out_specs=[pl.BlockSpec((r, w), rows),
                       pl.BlockSpec((1, N_HEADS, HEAD_DIM, HEAD_DIM), st)],
            out_shape=[jax.ShapeDtypeStruct((n, w), F32),
                       jax.ShapeDtypeStruct((batch, N_HEADS, HEAD_DIM, HEAD_DIM), F32)],
            scratch_shapes=[pltpu.VMEM((N_HEADS, HEAD_DIM, HEAD_DIM), F32)],
            compiler_params=_cparams(("parallel", "arbitrary")),
            name="ret_prompt",
        )(x, cos, sin, f(np.tile(dmat, (nc, 1, 1))), f(np.tile(qdec, (nc, 1))), f(np.tile(kdec, (nc, 1))),
          f(gc), ng)
        return out, ns
    nseq = SAMPLE_SEQS
    r = nseq * c
    seg = np.arange(r) // c
    same = (seg[:, None] == seg[None, :])[None]
    dblk = np.where(same, np.tile(dmat, (1, nseq, nseq)), 0.0)
    rows = lambda i: (i, 0)
    const = lambda i: (0, 0)
    const3 = lambda i: (0, 0, 0)
    st = lambda i: (i, 0, 0, 0)
    sblk = (nseq, N_HEADS, HEAD_DIM, HEAD_DIM)
    out, ns = pl.pallas_call(
        functools.partial(_ret_sample_kernel, nseq=nseq, c=c),
        grid=(batch // nseq,),
        in_specs=[pl.BlockSpec((r, RET_COLS), rows),
                  pl.BlockSpec((r, w), const),
                  pl.BlockSpec((r, w), const),
                  pl.BlockSpec((N_HEADS, r, r), const3),
                  pl.BlockSpec((r, w), const),
                  pl.BlockSpec((r, w), const),
                  pl.BlockSpec((1, w), const),
                  pl.BlockSpec((1, w), const),
                  pl.BlockSpec(sblk, st)],
        out_specs=[pl.BlockSpec((r, w), rows), pl.BlockSpec(sblk, st)],
        out_shape=[jax.ShapeDtypeStruct((n, w), F32),
                   jax.ShapeDtypeStruct((batch, N_HEADS, HEAD_DIM, HEAD_DIM), F32)],
        compiler_params=_cparams(("parallel",)),
        name="ret_sample",
    )(x, jnp.tile(cos, (nseq, 1)), jnp.tile(sin, (nseq, 1)), f(dblk), f(np.tile(qdec, (nseq, 1))),
      f(np.tile(kdec, (nseq, 1))), f(gc), ng, state)
    return out, ns


def _unit_lower_inverse(a, eye, c):
    inv = eye - a
    x = a
    cover = 2
    while cover < c // 2:
        xm = x.astype(BF16)
        x = _bmm('bij,bjk->bik', xm, xm)
        inv = inv + _bmm('bij,bjk->bik', inv.astype(BF16), x.astype(BF16))
        cover *= 2
    resid = eye - inv - _bmm3('bij,bjk->bik', a, inv)
    return inv + _bmm('bij,bjk->bik', inv.astype(BF16), resid.astype(BF16))


def _gdn_gates(ab, alog_ref, dtb_ref, tri_l, tri_u):
    g_log = -jnp.exp(alog_ref[...]) * _softplus(ab + dtb_ref[...])
    beta = jax.nn.sigmoid(ab)
    hi = lax.Precision.HIGHEST
    g_col = _dot(tri_l.astype(F32), g_log, precision=hi)
    g_row = _dot_tn(g_log, tri_u.astype(F32), precision=hi)
    return g_col, g_row, beta


def _l2norm_heads(x, base):
    out = []
    for h in range(N_HEADS):
        xr = x[:, _head_slice(h, base)]
        out.append(xr * lax.rsqrt(jnp.sum(xr * xr, axis=-1, keepdims=True) + EPS))
    return out


def _gdn_prompt_kernel(x_ref, cw_ref, alog_ref, dtb_ref, ng_ref, out_ref, nconv_ref, ns_ref, ext, s_scr,
                       *, r, c):
    t = pl.program_id(1)
    last = pl.num_programs(1) - 1
    w = GDN_WIDTH
    hist = SUBLANES - (CONV_W - 1)
    nc = r // c

    @pl.when(t == 0)
    def _():
        ext[0:SUBLANES, :] = jnp.zeros((SUBLANES, 3 * w), F32)
        s_scr[...] = jnp.zeros(s_scr.shape, F32)

    qkv = _silu(_conv_unit(ext, x_ref[:, 0:3 * w], cw_ref, r))

    @pl.when(t == last)
    def _():
        nconv_ref[0] = ext[r + hist:r + SUBLANES, :]

    ext[0:SUBLANES, :] = ext[r:r + SUBLANES, :]
    tri_l, _, tri_u = _segment_masks(r, c)
    g_col, g_row, beta = _gdn_gates(x_ref[:, 4 * w:4 * w + LANES], alog_ref, dtb_ref, tri_l, tri_u)
    qn = [a * (HEAD_DIM ** -0.5) for a in _l2norm_heads(qkv, 0)]
    kn = _l2norm_heads(qkv, w)
    rows = lambda n: slice(n * c, (n + 1) * c)

    def stack(fn):
        return jnp.stack([fn(n, h) for n in range(nc) for h in range(N_HEADS)], axis=0)

    q = stack(lambda n, h: qn[h][rows(n)])
    k = stack(lambda n, h: kn[h][rows(n)])
    v = stack(lambda n, h: qkv[rows(n), _head_slice(h, 2 * w)])
    gi = stack(lambda n, h: g_col[rows(n), h:h + 1])
    gj = stack(lambda n, h: g_row[h:h + 1, rows(n)])
    bi = stack(lambda n, h: beta[rows(n), GDN_BETA_LANE + h:GDN_BETA_LANE + h + 1])
    gl = stack(lambda n, h: g_col[(n + 1) * c - 1:(n + 1) * c, h:h + 1])
    lower, strict, _ = _segment_masks(c, c)
    eye = (lower & jnp.logical_not(strict)).astype(F32)[None]
    decay = jnp.exp(jnp.where(lower[None], gi - gj, -jnp.inf))
    kb = k.astype(BF16)
    a_mat = jnp.where(strict[None], bi * _bmm('bid,bjd->bij', kb, kb) * decay, 0.0)
    tm = _unit_lower_inverse(a_mat, eye, c)
    eg = jnp.exp(gi)
    rhs = jnp.concatenate([bi * v, (bi * eg) * k], axis=2)
    uw = _bmm('bij,bjk->bik', tm.astype(BF16), rhs.astype(BF16))
    uu = uw[:, :, 0:HEAD_DIM]
    pm = (_bmm('bid,bjd->bij', q.astype(BF16), kb) * decay).astype(BF16)
    wq = jnp.concatenate([uw[:, :, HEAD_DIM:2 * HEAD_DIM], q * eg], axis=1).astype(BF16)
    kd = (k * jnp.exp(gl - gi)).astype(BF16)
    egl = jnp.exp(gl)
    z = x_ref[:, 3 * w:4 * w]
    s = s_scr[...]
    for n in range(nc):
        ps = slice(n * N_HEADS, (n + 1) * N_HEADS)
        wqs = _bmm('hid,hde->hie', wq[ps], s.astype(BF16))
        vn = (uu[ps] - wqs[:, 0:c]).astype(BF16)
        o = wqs[:, c:2 * c] + _bmm('hij,hje->hie', pm[ps], vn)
        s = egl[ps] * s + _bmm('hid,hie->hde', kd[ps], vn)
        on = _head_rms(o) * ng_ref[...]
        for h in range(N_HEADS):
            hs = _head_slice(h)
            out_ref[rows(n), hs] = on[h] * _silu(z[rows(n), hs])
    s_scr[...] = s

    @pl.when(t == last)
    def _():
        ns_ref[0] = s


def _gdn_sample_kernel(x_ref, cw_ref, alog_ref, dtb_ref, ng_ref, buf_ref, s0_ref,
                       out_ref, nconv_ref, ns_ref, ext, y_scr, *, nseq, c):
    w = GDN_WIDTH
    hist = SUBLANES - (CONV_W - 1)
    r = nseq * c

    def conv_body(u, carry_):
        r0 = pl.multiple_of(u * c, SUBLANES)
        ext[hist:SUBLANES, :] = buf_ref[u]
        y_scr[pl.ds(r0, c), :] = _conv_unit(ext, x_ref[pl.ds(r0, c), 0:3 * w], cw_ref, c)
        nconv_ref[u] = ext[c + hist:c + SUBLANES, :]
        return carry_

    lax.fori_loop(0, nseq, conv_body, 0)
    qkv = _silu(y_scr[...])
    lower, strict, upper = _segment_masks(r, c)
    g_col, g_row, beta = _gdn_gates(x_ref[:, 4 * w:4 * w + LANES], alog_ref, dtb_ref, lower, upper)
    g_seq = g_col.reshape(nseq, c, LANES)
    g_end = g_seq[:, c - 1:c, :]
    g_end_rows = jnp.broadcast_to(g_end, (nseq, c, LANES)).reshape(r, LANES)
    qn = [a * (HEAD_DIM ** -0.5) for a in _l2norm_heads(qkv, 0)]
    kn = _l2norm_heads(qkv, w)
    heads = range(N_HEADS)
    hstack = lambda fn: jnp.stack([fn(h) for h in heads], axis=0)
    q = hstack(lambda h: qn[h])
    k = hstack(lambda h: kn[h])
    v = hstack(lambda h: qkv[:, _head_slice(h, 2 * w)])
    gi = hstack(lambda h: g_col[:, h:h + 1])
    gj = hstack(lambda h: g_row[h:h + 1, :])
    bi = hstack(lambda h: beta[:, GDN_BETA_LANE + h:GDN_BETA_LANE + h + 1])
    ge = hstack(lambda h: g_end_rows[:, h:h + 1])
    eye = (lower & jnp.logical_not(strict)).astype(F32)[None]
    decay = jnp.exp(jnp.where(lower[None], gi - gj, -jnp.inf))
    kb = k.astype(BF16)
    a_mat = jnp.where(strict[None], bi * _bmm('bid,bjd->bij', kb, kb) * decay, 0.0)
    tm = _unit_lower_inverse(a_mat, eye, c)
    eg = jnp.exp(gi)
    rhs = jnp.concatenate([bi * v, (bi * eg) * k], axis=2)
    uw = _bmm('bij,bjk->bik', tm.astype(BF16), rhs.astype(BF16))
    pm = (_bmm('bid,bjd->bij', q.astype(BF16), kb) * decay).astype(BF16)
    qg = q * eg
    kd = k * jnp.exp(ge - gi)
    per_seq = lambda a: a.reshape(nseq, c, HEAD_DIM)
    s0 = [s0_ref[:, h] for h in heads]
    wqs = [_bmm('bid,bde->bie',
                jnp.concatenate([per_seq(uw[h][:, HEAD_DIM:2 * HEAD_DIM]), per_seq(qg[h])], axis=1), s0[h])
           for h in heads]
    vn = [per_seq(uw[h][:, 0:HEAD_DIM]) - wqs[h][:, 0:c] for h in heads]
    o = [wqs[h][:, c:2 * c].reshape(r, HEAD_DIM)
         + _dot(pm[h], vn[h].reshape(r, HEAD_DIM).astype(BF16)) for h in heads]
    upd = [_bmm('bid,bie->bde', per_seq(kd[h]), vn[h]) for h in heads]
    z = x_ref[:, 3 * w:4 * w]
    for h in heads:
        hs = _head_slice(h)
        ns_ref[:, h] = jnp.exp(g_end[:, :, h:h + 1]) * s0[h] + upd[h]
        out_ref[:, hs] = _head_rms(o[h]) * ng_ref[...] * _silu(z[:, hs])


def _gdn(x, cw, alog, dtb, ng, state, *, batch, seq):
    w = GDN_WIDTH
    n = batch * seq
    c = math.gcd(seq, CHUNK)
    if state is None:
        r = PROMPT_BLOCK
        nt = seq // r
        rows = lambda b, t: (b * nt + t, 0)
        const = lambda b, t: (0, 0)
        st3 = lambda b, t: (b, 0, 0)
        st4 = lambda b, t: (b, 0, 0, 0)
        return pl.pallas_call(
            functools.partial(_gdn_prompt_kernel, r=r, c=c),
            grid=(batch, nt),
            in_specs=[pl.BlockSpec((r, GDN_COLS), rows),
                      pl.BlockSpec((CONV_W, 3 * w), const),
                      pl.BlockSpec((1, LANES), const),
                      pl.BlockSpec((1, LANES), const),
                      pl.BlockSpec((1, HEAD_DIM), const)],
            out_specs=[pl.BlockSpec((r, w), rows),
                       pl.BlockSpec((1, CONV_W - 1, 3 * w), st3),
                       pl.BlockSpec((1, N_HEADS, HEAD_DIM, HEAD_DIM), st4)],
            out_shape=[jax.ShapeDtypeStruct((n, w), F32),
                       jax.ShapeDtypeStruct((batch, CONV_W - 1, 3 * w), F32),
                       jax.ShapeDtypeStruct((batch, N_HEADS, HEAD_DIM, HEAD_DIM), F32)],
            scratch_shapes=[pltpu.VMEM((r + SUBLANES, 3 * w), F32),
                            pltpu.VMEM((N_HEADS, HEAD_DIM, HEAD_DIM), F32)],
            compiler_params=_cparams(("parallel", "arbitrary")),
            name="gdn_prompt",
        )(x, cw, alog, dtb, ng)
    nseq = SAMPLE_SEQS
    r = nseq * c
    rows = lambda i: (i, 0)
    const = lambda i: (0, 0)
    st3 = lambda i: (i, 0, 0)
    st4 = lambda i: (i, 0, 0, 0)
    cblk = (nseq, CONV_W - 1, 3 * w)
    sblk = (nseq, N_HEADS, HEAD_DIM, HEAD_DIM)
    return pl.pallas_call(
        functools.partial(_gdn_sample_kernel, nseq=nseq, c=c),
        grid=(batch // nseq,),
        in_specs=[pl.BlockSpec((r, GDN_COLS), rows),
                  pl.BlockSpec((CONV_W, 3 * w), const),
                  pl.BlockSpec((1, LANES), const),
                  pl.BlockSpec((1, LANES), const),
                  pl.BlockSpec((1, HEAD_DIM), const),
                  pl.BlockSpec(cblk, st3),
                  pl.BlockSpec(sblk, st4)],
        out_specs=[pl.BlockSpec((r, w), rows), pl.BlockSpec(cblk, st3), pl.BlockSpec(sblk, st4)],
        out_shape=[jax.ShapeDtypeStruct((n, w), F32),
                   jax.ShapeDtypeStruct((batch, CONV_W - 1, 3 * w), F32),
                   jax.ShapeDtypeStruct((batch, N_HEADS, HEAD_DIM, HEAD_DIM), F32)],
        scratch_shapes=[pltpu.VMEM((c + SUBLANES, 3 * w), F32), pltpu.VMEM((r, 3 * w), F32)],
        compiler_params=_cparams(("parallel",)),
        name="gdn_sample",
    )(x, cw, alog, dtb, ng, *state)


def _block_diag(wb):
    out = jnp.zeros((LRU_WIDTH, LRU_WIDTH), wb.dtype)
    for n in range(LRU_BLOCKS):
        s = slice(n * LRU_BLOCK, (n + 1) * LRU_BLOCK)
        out = out.at[s, s].set(wb[n])
    return out


def _pad_lanes(v, offset=0):
    return jnp.zeros((1, LANES), F32).at[0, offset:offset + v.shape[0]].set(v.astype(F32))


def _group_cols(w):
    w = w.reshape(N_GROUPS, EXPERTS_PER_GROUP, D_MODEL, D_EXPERT)
    return jnp.transpose(w, (0, 2, 1, 3)).reshape(N_GROUPS, D_MODEL, D_GROUP).astype(BF16)


def _prep_layer(W, l):
    row = lambda v: v.reshape(1, -1).astype(F32)
    pad = IN_COLS - W['w_in'].shape[2]
    wr = jnp.zeros((D_MODEL, LANES), F32)
    wr = wr.at[:, ROUTER_GROUP_LANE:ROUTER_GROUP_LANE + N_GROUPS].set(W['w_router_group'][l])
    wr = wr.at[:, ROUTER_EXPERT_LANE:ROUTER_EXPERT_LANE + N_EXPERTS].set(W['w_router_expert'][l])
    br = jnp.zeros((1, LANES), F32)
    br = br.at[0, ROUTER_GROUP_LANE:ROUTER_GROUP_LANE + N_GROUPS].set(W['b_router_group'][l])
    br = br.at[0, ROUTER_EXPERT_LANE:ROUTER_EXPERT_LANE + N_EXPERTS].set(W['b_router_expert'][l])
    return dict(
        norm_mix_g=row(W['norm_mix_g'][l]),
        w_in=jnp.pad(W['w_in'][l], ((0, 0), (0, pad))).astype(BF16),
        lru_conv_w=W['lru_conv_w'][l], lru_conv_b=row(W['lru_conv_b'][l]),
        lru_wa=_block_diag(W['lru_wa'][l]).astype(BF16), lru_ba=row(W['lru_ba'][l]),
        lru_wx=_block_diag(W['lru_wx'][l]).astype(BF16), lru_bx=row(W['lru_bx'][l]),
        lru_lambda=row(W['lru_lambda'][l]),
        ret_norm_g=row(W['ret_norm_g'][l]),
        gdn_conv_w=W['gdn_conv_w'][l],
        gdn_a_log=_pad_lanes(W['gdn_a_log'][l]), gdn_dt_bias=_pad_lanes(W['gdn_dt_bias'][l]),
        gdn_norm_g=row(W['gdn_norm_g'][l]),
        w_out=W['w_out'][l].astype(BF16),
        norm_ffn_g=row(W['norm_ffn_g'][l]),
        w_router=wr.astype(BF16), b_router=br,
        w_expert_gate=_group_cols(W['w_expert_gate'][l]),
        w_expert_up=_group_cols(W['w_expert_up'][l]),
        w_expert_down=W['w_expert_down'][l].reshape(N_GROUPS, D_GROUP, D_MODEL).astype(BF16),
        norm_pe_g=row(W['norm_pe_g'][l]),
        w_pe=W['w_pe'][l].astype(BF16),
        w_pe_gate=W['w_pe_gate'][l].astype(BF16),
    )


def _trunk(x, p, states, pos0, layers, final_g):
    batch, seq, _ = x.shape
    n = batch * seq
    h = x.reshape(n, D_MODEL)
    new = ([], [], [], [], [])
    for l, L in enumerate(layers):
        if states is None:
            st_lru = st_ret = st_gdn = None
        else:
            st_lru = (states[0][l], states[1][l])
            st_ret = states[2][l]
            st_gdn = (states[3][l], states[4][l])
        lru_x, ret_x, gdn_x = _in_proj(h, L['norm_mix_g'], L['w_in'])
        out_a, n_lru_conv, n_lru_h = _lru(
            lru_x, L['lru_conv_w'], L['lru_conv_b'], L['lru_wa'], L['lru_ba'], L['lru_wx'],
            L['lru_bx'], L['lru_lambda'], st_lru, batch=batch, seq=seq)
        out_b, n_ret = _ret(ret_x, L['ret_norm_g'], st_ret, batch=batch, seq=seq, pos0=pos0)
        out_c, n_gdn_conv, n_gdn = _gdn(
            gdn_x, L['gdn_conv_w'], L['gdn_a_log'], L['gdn_dt_bias'], L['gdn_norm_g'], st_gdn,
            batch=batch, seq=seq)
        h1, xn, gates = _out_router(h, out_a, out_b, out_c, L['w_out'], L['norm_ffn_g'],
                                    L['w_router'], L['b_router'])
        y = _moe(xn, gates, L['w_expert_gate'], L['w_expert_up'], L['w_expert_down'])
        h = _pe(h1, y, p[l].reshape(n, PE_DIM), L['norm_pe_g'], L['w_pe_gate'], L['w_pe'], final_g,
                final=(l == len(layers) - 1))
        for lst, s in zip(new, (n_lru_conv, n_lru_h, n_ret, n_gdn_conv, n_gdn)):
            lst.append(s)
    return h.reshape(batch, seq, D_MODEL), tuple(jnp.stack(lst) for lst in new)


def kernel(x_prompt, x_sample, p_prompt, p_sample, state_lru_conv, state_lru_h, state_ret, state_gdn_conv, state_gdn, norm_mix_g, w_in, lru_conv_w, lru_conv_b, lru_wa, lru_ba, lru_wx, lru_bx, lru_lambda, ret_norm_g, gdn_conv_w, gdn_a_log, gdn_dt_bias, gdn_norm_g, w_out, norm_ffn_g, w_router_group, b_router_group, w_router_expert, b_router_expert, w_expert_gate, w_expert_up, w_expert_down, norm_pe_g, w_pe, w_pe_gate, final_norm_g):
    W = dict(norm_mix_g=norm_mix_g, w_in=w_in, lru_conv_w=lru_conv_w, lru_conv_b=lru_conv_b,
             lru_wa=lru_wa, lru_ba=lru_ba, lru_wx=lru_wx, lru_bx=lru_bx, lru_lambda=lru_lambda,
             ret_norm_g=ret_norm_g, gdn_conv_w=gdn_conv_w, gdn_a_log=gdn_a_log, gdn_dt_bias=gdn_dt_bias,
             gdn_norm_g=gdn_norm_g, w_out=w_out, norm_ffn_g=norm_ffn_g, w_router_group=w_router_group,
             b_router_group=b_router_group, w_router_expert=w_router_expert, b_router_expert=b_router_expert,
             w_expert_gate=w_expert_gate, w_expert_up=w_expert_up, w_expert_down=w_expert_down,
             norm_pe_g=norm_pe_g, w_pe=w_pe, w_pe_gate=w_pe_gate)
    layers = [_prep_layer(W, l) for l in range(DEPTH)]
    final_g = final_norm_g.reshape(1, D_MODEL).astype(F32)
    y_p, st_p = _trunk(x_prompt, p_prompt, None, 0, layers, final_g)
    sample_states = (state_lru_conv, state_lru_h, state_ret, state_gdn_conv, state_gdn)
    y_s, st_s = _trunk(x_sample, p_sample, sample_states, PAST_LEN, layers, final_g)
    return (y_p, y_s) + st_p + st_s
```

```python
import functools
import math

import jax
import jax.numpy as jnp
import numpy as np
from jax import lax
from jax.experimental import pallas as pl
from jax.experimental.pallas import tpu as pltpu

F32 = jnp.float32
BF16 = jnp.bfloat16

D_MODEL = 1024
DEPTH = 2
PAST_LEN = 16384
HEAD_DIM = 64
HALF = HEAD_DIM // 2
LRU_WIDTH = 256
LRU_BLOCKS = 4
LRU_BLOCK = 64
LRU_C = 8.0
RET_WIDTH = 384
GDN_WIDTH = 384
N_HEADS = 6
CONV_W = 4
CHUNK = 64
ROPE_BASE = 10000.0
PE_DIM = 256
N_GROUPS = 4
EXPERTS_PER_GROUP = 4
N_EXPERTS = 16
D_EXPERT = 256
EPS = 1e-6

LANES = 128
SUBLANES = 8
VMEM_LIMIT = 48 * 1024 * 1024

LRU_COLS = 2 * LRU_WIDTH
RET_COLS = 4 * RET_WIDTH
GDN_COLS = 4 * GDN_WIDTH + LANES
IN_COLS = LRU_COLS + RET_COLS + GDN_COLS
ROUTER_GROUP_LANE = 0
ROUTER_EXPERT_LANE = N_GROUPS
GDN_BETA_LANE = N_HEADS

D_GROUP = EXPERTS_PER_GROUP * D_EXPERT

TOKEN_TILE = 512
MOE_TILE = 1024
PROMPT_BLOCK = 256
SAMPLE_SEQS = 16


def _cparams(sem):
    return pltpu.CompilerParams(dimension_semantics=sem, vmem_limit_bytes=VMEM_LIMIT)


def _rmsnorm(x, g):
    return x * lax.rsqrt(jnp.mean(x * x, axis=-1, keepdims=True) + EPS) * g


def _dot(a, b, **kw):
    return jnp.dot(a, b, preferred_element_type=F32, **kw)


def _dot_nt(a, b):
    return lax.dot_general(a, b, (((1,), (1,)), ((), ())), preferred_element_type=F32)


def _dot_tn(a, b, **kw):
    return lax.dot_general(a, b, (((0,), (0,)), ((), ())), preferred_element_type=F32, **kw)


def _bmm(spec, a, b):
    return jnp.einsum(spec, a, b, preferred_element_type=F32)


def _bmm3(spec, a, b):
    a_hi = a.astype(BF16)
    a_lo = (a - a_hi.astype(F32)).astype(BF16)
    b_hi = b.astype(BF16)
    b_lo = (b - b_hi.astype(F32)).astype(BF16)
    return _bmm(spec, a_hi, b_hi) + (_bmm(spec, a_hi, b_lo) + _bmm(spec, a_lo, b_hi))


def _softplus(x):
    return jnp.maximum(x, 0.0) + jnp.log1p(jnp.exp(-jnp.abs(x)))


def _silu(x):
    return x * jax.nn.sigmoid(x)


def _gelu_tanh(x):
    return 0.5 * x * (1.0 + jnp.tanh(math.sqrt(2.0 / math.pi) * (x + 0.044715 * (x * x * x))))


def _in_proj_kernel(h_ref, g_ref, w_ref, lru_ref, ret_ref, gdn_ref):
    xn = _rmsnorm(h_ref[...], g_ref[...]).astype(BF16)
    lru_ref[...] = _dot(xn, w_ref[:, 0:LRU_COLS])
    ret_ref[...] = _dot(xn, w_ref[:, LRU_COLS:LRU_COLS + RET_COLS])
    gdn_ref[...] = _dot(xn, w_ref[:, LRU_COLS + RET_COLS:IN_COLS])


def _in_proj(h, g, w):
    n = h.shape[0]
    tm = TOKEN_TILE
    row = lambda i: (i, 0)
    const = lambda i: (0, 0)
    return pl.pallas_call(
        _in_proj_kernel,
        grid=(n // tm,),
        in_specs=[pl.BlockSpec((tm, D_MODEL), row),
                  pl.BlockSpec((1, D_MODEL), const),
                  pl.BlockSpec((D_MODEL, IN_COLS), const)],
        out_specs=[pl.BlockSpec((tm, LRU_COLS), row),
                   pl.BlockSpec((tm, RET_COLS), row),
                   pl.BlockSpec((tm, GDN_COLS), row)],
        out_shape=[jax.ShapeDtypeStruct((n, LRU_COLS), F32),
                   jax.ShapeDtypeStruct((n, RET_COLS), F32),
                   jax.ShapeDtypeStruct((n, GDN_COLS), F32)],
        compiler_params=_cparams(("parallel",)),
        name="in_proj",
    )(h, g, w)


def _route(logits):
    lane = lax.broadcasted_iota(jnp.int32, logits.shape, 1).astype(F32)
    neg = -jnp.inf
    far = float(LANES)
    gmask = lane < N_GROUPS
    gl = jnp.where(gmask, logits, neg)
    gmax = jnp.max(gl, axis=-1, keepdims=True)
    gsel = jnp.min(jnp.where(gl == gmax, lane, far), axis=-1, keepdims=True)
    gsum = jnp.sum(jnp.where(gmask, jnp.exp(logits - gmax), 0.0), axis=-1, keepdims=True)
    g_w = 1.0 / gsum
    lo = ROUTER_EXPERT_LANE + EXPERTS_PER_GROUP * gsel
    el = jnp.where(lane >= lo, jnp.where(lane < lo + EXPERTS_PER_GROUP, logits, neg), neg)
    m1 = jnp.max(el, axis=-1, keepdims=True)
    i1 = jnp.min(jnp.where(el == m1, lane, far), axis=-1, keepdims=True)
    el2 = jnp.where(lane == i1, neg, el)
    m2 = jnp.max(el2, axis=-1, keepdims=True)
    i2 = jnp.min(jnp.where(el2 == m2, lane, far), axis=-1, keepdims=True)
    e2 = jnp.exp(m2 - m1)
    w1 = g_w / (1.0 + e2)
    w2 = w1 * e2
    return jnp.where(lane == i1, w1, 0.0) + jnp.where(lane == i2, w2, 0.0)


def _out_router_kernel(h_ref, a_ref, b_ref, c_ref, wo_ref, g_ref, wr_ref, br_ref,
                       h1_ref, xn_ref, gates_ref):
    o1 = LRU_WIDTH
    o2 = LRU_WIDTH + RET_WIDTH
    mix = (_dot(a_ref[...].astype(BF16), wo_ref[0:o1, :])
           + _dot(b_ref[...].astype(BF16), wo_ref[o1:o2, :])
           + _dot(c_ref[...].astype(BF16), wo_ref[o2:D_MODEL, :]))
    h1 = h_ref[...] + mix
    h1_ref[...] = h1
    xn = _rmsnorm(h1, g_ref[...]).astype(BF16)
    xn_ref[...] = xn
    gates_ref[...] = _route(_dot(xn, wr_ref[...]) + br_ref[...])


def _out_router(h, a, b, c, wo, g, wr, br):
    n = h.shape[0]
    tm = TOKEN_TILE
    row = lambda i: (i, 0)
    const = lambda i: (0, 0)
    return pl.pallas_call(
        _out_router_kernel,
        grid=(n // tm,),
        in_specs=[pl.BlockSpec((tm, D_MODEL), row),
                  pl.BlockSpec((tm, LRU_WIDTH), row),
                  pl.BlockSpec((tm, RET_WIDTH), row),
                  pl.BlockSpec((tm, GDN_WIDTH), row),
                  pl.BlockSpec((D_MODEL, D_MODEL), const),
                  pl.BlockSpec((1, D_MODEL), const),
                  pl.BlockSpec((D_MODEL, LANES), const),
                  pl.BlockSpec((1, LANES), const)],
        out_specs=[pl.BlockSpec((tm, D_MODEL), row),
                   pl.BlockSpec((tm, D_MODEL), row),
                   pl.BlockSpec((tm, LANES), row)],
        out_shape=[jax.ShapeDtypeStruct((n, D_MODEL), F32),
                   jax.ShapeDtypeStruct((n, D_MODEL), BF16),
                   jax.ShapeDtypeStruct((n, LANES), F32)],
        compiler_params=_cparams(("parallel",)),
        name="out_router",
    )(h, a, b, c, wo, g, wr, br)


def _moe_kernel(xn_ref, gates_ref, h1_ref, wg_ref, wu_ref, wd_ref, out_ref):
    grp = pl.program_id(1)

    @pl.when(grp == 0)
    def _():
        out_ref[...] = h1_ref[...]

    xn = xn_ref[...]
    hg = _dot(xn, wg_ref[...])
    hu = _dot(xn, wu_ref[...])
    gates = gates_ref[...]
    lane = lax.broadcasted_iota(jnp.int32, gates.shape, 1)
    base = ROUTER_EXPERT_LANE + EXPERTS_PER_GROUP * grp
    parts = []
    for j in range(EXPERTS_PER_GROUP):
        sl = slice(j * D_EXPERT, (j + 1) * D_EXPERT)
        gcol = jnp.sum(jnp.where(lane == base + j, gates, 0.0), axis=-1, keepdims=True)
        parts.append((_silu(hg[:, sl]) * hu[:, sl] * gcol).astype(BF16))
    out_ref[...] += _dot(jnp.concatenate(parts, axis=1), wd_ref[...])


def _moe(xn, gates, h1, wg, wu, wd):
    n = xn.shape[0]
    tm = MOE_TILE if n % MOE_TILE == 0 else TOKEN_TILE
    row = lambda i, e: (i, 0)
    return pl.pallas_call(
        _moe_kernel,
        grid=(n // tm, N_GROUPS),
        in_specs=[pl.BlockSpec((tm, D_MODEL), row),
                  pl.BlockSpec((tm, LANES), row),
                  pl.BlockSpec((tm, D_MODEL), row),
                  pl.BlockSpec((None, D_MODEL, D_GROUP), lambda i, e: (e, 0, 0)),
                  pl.BlockSpec((None, D_MODEL, D_GROUP), lambda i, e: (e, 0, 0)),
                  pl.BlockSpec((None, D_GROUP, D_MODEL), lambda i, e: (e, 0, 0))],
        out_specs=pl.BlockSpec((tm, D_MODEL), row),
        out_shape=jax.ShapeDtypeStruct((n, D_MODEL), F32),
        compiler_params=_cparams(("parallel", "arbitrary")),
        name="moe",
    )(xn, gates, h1, wg, wu, wd)


def _pe_kernel(h_ref, p_ref, g_ref, wgate_ref, wpe_ref, fg_ref, out_ref, *, final):
    h = h_ref[...]
    xn = _rmsnorm(h, g_ref[...]).astype(BF16)
    gate = jax.nn.sigmoid(_dot(xn, wgate_ref[...]))
    pe = _dot(p_ref[...].astype(BF16), wpe_ref[...])
    h3 = h + pe * gate
    out_ref[...] = _rmsnorm(h3, fg_ref[...]) if final else h3


def _pe(h, p, g, wgate, wpe, fg, final):
    n = h.shape[0]
    tm = TOKEN_TILE
    row = lambda i: (i, 0)
    const = lambda i: (0, 0)
    return pl.pallas_call(
        functools.partial(_pe_kernel, final=final),
        grid=(n // tm,),
        in_specs=[pl.BlockSpec((tm, D_MODEL), row),
                  pl.BlockSpec((tm, PE_DIM), row),
                  pl.BlockSpec((1, D_MODEL), const),
                  pl.BlockSpec((D_MODEL, D_MODEL), const),
                  pl.BlockSpec((PE_DIM, D_MODEL), const),
                  pl.BlockSpec((1, D_MODEL), const)],
        out_specs=pl.BlockSpec((tm, D_MODEL), row),
        out_shape=jax.ShapeDtypeStruct((n, D_MODEL), F32),
        compiler_params=_cparams(("parallel",)),
        name="pe_gate",
    )(h, p, g, wgate, wpe, fg)


def _conv_unit(ext, x, w_ref, c):
    ext[SUBLANES:SUBLANES + c, :] = x
    y = w_ref[CONV_W - 1:CONV_W, :] * x
    for j in range(CONV_W - 1):
        off = SUBLANES - (CONV_W - 1) + j
        y = y + w_ref[j:j + 1, :] * ext[off:off + c, :]
    return y


def _lru_kernel(*refs, c, units, carry):
    if carry:
        (x_ref, cw_ref, cb_ref, wa_ref, ba_ref, wx_ref, bx_ref, lam_ref,
         out_ref, nconv_ref, nh_ref, ext, y_scr, a_scr, b_scr, hcar) = refs
    else:
        (x_ref, cw_ref, cb_ref, wa_ref, ba_ref, wx_ref, bx_ref, lam_ref, buf_ref, h0_ref,
         out_ref, nconv_ref, nh_ref, ext, y_scr, a_scr, b_scr) = refs
    w = LRU_WIDTH
    hist = SUBLANES - (CONV_W - 1)
    if carry:
        t = pl.program_id(1)
        last = pl.num_programs(1) - 1

        @pl.when(t == 0)
        def _():
            ext[0:SUBLANES, :] = jnp.zeros((SUBLANES, w), F32)
            hcar[...] = jnp.zeros((1, w), F32)

    def conv_body(u, carry_):
        r0 = pl.multiple_of(u * c, SUBLANES)
        if not carry:
            ext[hist:SUBLANES, :] = buf_ref[u]
        x = x_ref[pl.ds(r0, c), 0:w]
        y_scr[pl.ds(r0, c), :] = _conv_unit(ext, x, cw_ref, c) + cb_ref[...]
        if carry:
            @pl.when(t == last)
            def _():
                nconv_ref[0] = ext[c + hist:c + SUBLANES, :]
            ext[0:SUBLANES, :] = ext[c:c + SUBLANES, :]
        else:
            nconv_ref[u] = ext[c + hist:c + SUBLANES, :]
        return carry_

    lax.fori_loop(0, units, conv_body, 0)

    y = y_scr[...]
    yb = y.astype(BF16)
    r = jax.nn.sigmoid(_dot(yb, wa_ref[...]) + ba_ref[...])
    i = jax.nn.sigmoid(_dot(yb, wx_ref[...]) + bx_ref[...])
    log_a = (-LRU_C) * r * _softplus(-lam_ref[...])
    th = jnp.tanh(log_a)
    a_scr[...] = jnp.exp(log_a)
    b_scr[...] = jnp.sqrt(-2.0 * th / (1.0 - th)) * (i * y)

    def scan_body(u, carry_):
        r0 = pl.multiple_of(u * c, SUBLANES)
        a = a_scr[pl.ds(r0, c), :]
        b = b_scr[pl.ds(r0, c), :]
        rowi = lax.broadcasted_iota(jnp.int32, (c, w), 0)
        d = 1
        while d < c:
            keep = rowi >= d
            a_s = pltpu.roll(a, d, axis=0)
            b_s = pltpu.roll(b, d, axis=0)
            b = jnp.where(keep, a * b_s + b, b)
            a = jnp.where(keep, a * a_s, a)
            d *= 2
        h0 = hcar[...] if carry else h0_ref[u]
        h = a * h0 + b
        gate = x_ref[pl.ds(r0, c), w:2 * w]
        out_ref[pl.ds(r0, c), :] = h * _gelu_tanh(gate)
        if carry:
            hcar[...] = h[c - 1:c, :]
        else:
            nh_ref[u] = h[c - 1:c, :]
        return carry_

    lax.fori_loop(0, units, scan_body, 0)
    if carry:
        @pl.when(t == last)
        def _():
            nh_ref[0] = hcar[...]


def _lru(x, cw, cb, wa, ba, wx, bx, lam, state, *, batch, seq):
    w = LRU_WIDTH
    carry = state is None
    if carry:
        c, units = PROMPT_BLOCK, 1
        nt = seq // c
        grid = (batch, nt)
        rows = lambda b, t: (b * nt + t, 0)
        const = lambda b, t: (0, 0)
        st = lambda b, t: (b, 0, 0)
        sem = ("parallel", "arbitrary")
        state_in, state_specs = [], []
    else:
        c, units = seq, batch
        grid = (1,)
        rows = lambda i: (0, 0)
        const = lambda i: (0, 0)
        st = lambda i: (0, 0, 0)
        sem = ("arbitrary",)
        state_in = [state[0], state[1].reshape(batch, 1, w)]
        state_specs = [pl.BlockSpec((units, CONV_W - 1, w), st), pl.BlockSpec((units, 1, w), st)]
    n = batch * seq
    blk = c * units
    scratch = [pltpu.VMEM((c + SUBLANES, w), F32), pltpu.VMEM((blk, w), F32),
               pltpu.VMEM((blk, w), F32), pltpu.VMEM((blk, w), F32)]
    if carry:
        scratch.append(pltpu.VMEM((1, w), F32))
    out, nconv, nh = pl.pallas_call(
        functools.partial(_lru_kernel, c=c, units=units, carry=carry),
        grid=grid,
        in_specs=[pl.BlockSpec((blk, LRU_COLS), rows),
                  pl.BlockSpec((CONV_W, w), const),
                  pl.BlockSpec((1, w), const),
                  pl.BlockSpec((w, w), const),
                  pl.BlockSpec((1, w), const),
                  pl.BlockSpec((w, w), const),
                  pl.BlockSpec((1, w), const),
                  pl.BlockSpec((1, w), const)] + state_specs,
        out_specs=[pl.BlockSpec((blk, w), rows),
                   pl.BlockSpec((units, CONV_W - 1, w), st),
                   pl.BlockSpec((units, 1, w), st)],
        out_shape=[jax.ShapeDtypeStruct((n, w), F32),
                   jax.ShapeDtypeStruct((batch, CONV_W - 1, w), F32),
                   jax.ShapeDtypeStruct((batch, 1, w), F32)],
        scratch_shapes=scratch,
        compiler_params=_cparams(sem),
        name="lru_prompt" if carry else "lru_sample",
    )(x, cw, cb, wa, ba, wx, bx, lam, *state_in)
    return out, nconv, nh.reshape(batch, w)


def _rope(x, cos, sin):
    parts = []
    for j in range(x.shape[1] // LANES):
        sl = slice(j * LANES, (j + 1) * LANES)
        xs = x[:, sl]
        lane = lax.broadcasted_iota(jnp.int32, xs.shape, 1)
        swapped = jnp.where((lane & HALF) == 0,
                            pltpu.roll(xs, LANES - HALF, axis=1), pltpu.roll(xs, HALF, axis=1))
        parts.append(xs * cos[:, sl] + swapped * sin[:, sl])
    return jnp.concatenate(parts, axis=1)


def _segment_masks(r, c):
    ri = lax.broadcasted_iota(jnp.int32, (r, r), 0)
    ci = lax.broadcasted_iota(jnp.int32, (r, r), 1)
    if c == r:
        return ri >= ci, ri > ci, ri <= ci
    shift = c.bit_length() - 1
    same = lax.shift_right_logical(ri, shift) == lax.shift_right_logical(ci, shift)
    return same & (ri >= ci), same & (ri > ci), same & (ri <= ci)


def _head_slice(h, base=0):
    return slice(base + h * HEAD_DIM, base + (h + 1) * HEAD_DIM)


def _head_rms(o):
    return o * lax.rsqrt(jnp.mean(o * o, axis=-1, keepdims=True) + EPS)


def _ret_prompt_kernel(x_ref, cos_ref, sin_ref, dmat_ref, qdec_ref, kdec_ref, gc_ref, ng_ref,
                       out_ref, ns_ref, s_scr, *, r, c):
    t = pl.program_id(1)
    last = pl.num_programs(1) - 1

    @pl.when(t == 0)
    def _():
        s_scr[...] = jnp.zeros(s_scr.shape, F32)

    w = RET_WIDTH
    nc = r // c
    cos = cos_ref[...]
    sin = sin_ref[...]
    q = _rope(x_ref[:, 0:w], cos, sin) * (HEAD_DIM ** -0.5)
    k = _rope(x_ref[:, w:2 * w], cos, sin)
    v = x_ref[:, 2 * w:3 * w]
    qd = q * qdec_ref[...]
    kd = k * kdec_ref[...]
    rows = lambda n: slice(n * c, (n + 1) * c)

    def stack(a):
        return jnp.stack([a[rows(n), _head_slice(h)] for n in range(nc) for h in range(N_HEADS)], axis=0)

    qs, ks, vs = stack(q).astype(BF16), stack(k).astype(BF16), stack(v).astype(BF16)
    scores = _bmm('bid,bjd->bij', qs, ks) * dmat_ref[...]
    o_inner = _bmm('bij,bje->bie', scores.astype(BF16), vs)
    kv = _bmm('bid,bie->bde', stack(kd).astype(BF16), vs)
    gc = jnp.stack([gc_ref[:, _head_slice(h)] for h in range(N_HEADS)], axis=0)
    s = s_scr[...]
    starts = []
    for n in range(nc):
        starts.append(s)
        s = gc * s + kv[n * N_HEADS:(n + 1) * N_HEADS]
    s_scr[...] = s

    @pl.when(t == last)
    def _():
        ns_ref[0] = s

    o = o_inner + _bmm('bid,bde->bie', stack(qd).astype(BF16), jnp.concatenate(starts, axis=0).astype(BF16))
    on = _head_rms(o)
    g = x_ref[:, 3 * w:4 * w]
    for n in range(nc):
        for h in range(N_HEADS):
            hs = _head_slice(h)
            out_ref[rows(n), hs] = on[n * N_HEADS + h] * ng_ref[:, hs] * _silu(g[rows(n), hs])


def _ret_sample_kernel(x_ref, cos_ref, sin_ref, dmat_ref, qdec_ref, kdec_ref, gc_ref, ng_ref, s0_ref,
                       out_ref, ns_ref, *, nseq, c):
    w = RET_WIDTH
    cos = cos_ref[...]
    sin = sin_ref[...]
    q = _rope(x_ref[:, 0:w], cos, sin) * (HEAD_DIM ** -0.5)
    k = _rope(x_ref[:, w:2 * w], cos, sin)
    v = x_ref[:, 2 * w:3 * w]
    qd = q * qdec_ref[...]
    kd = k * kdec_ref[...]
    g = x_ref[:, 3 * w:4 * w]
    per_seq = lambda a: a.reshape(nseq, c, HEAD_DIM)
    heads = range(N_HEADS)
    vb = [v[:, _head_slice(h)].astype(BF16) for h in heads]
    scores = [_dot_nt(q[:, _head_slice(h)].astype(BF16), k[:, _head_slice(h)].astype(BF16)) * dmat_ref[h]
              for h in heads]
    o_inner = [_dot(scores[h].astype(BF16), vb[h]) for h in heads]
    s0 = [s0_ref[:, h] for h in heads]
    o_cross = [_bmm('bqd,bde->bqe', per_seq(qd[:, _head_slice(h)]), s0[h]) for h in heads]
    kv = [_bmm('bkd,bke->bde', per_seq(kd[:, _head_slice(h)]), per_seq(v[:, _head_slice(h)])) for h in heads]
    for h in heads:
        hs = _head_slice(h)
        ns_ref[:, h] = gc_ref[:, hs] * s0[h] + kv[h]
        o = o_inner[h] + o_cross[h].reshape(nseq * c, HEAD_DIM)
        out_ref[:, hs] = _head_rms(o) * ng_ref[:, hs] * _silu(g[:, hs])


def _rope_tables(pos0, seq):
    inv = np.exp(-math.log(ROPE_BASE) * np.arange(HALF, dtype=np.float64) / HALF)
    ang = (pos0 + np.arange(seq, dtype=np.float64))[:, None] * inv[None]
    cos = np.tile(np.concatenate([np.cos(ang), np.cos(ang)], axis=1), (1, N_HEADS))
    sin = np.tile(np.concatenate([-np.sin(ang), np.sin(ang)], axis=1), (1, N_HEADS))
    return jnp.asarray(cos, dtype=F32), jnp.asarray(sin, dtype=F32)


def _ret_decay_tables(c):
    hh = np.arange(N_HEADS, dtype=np.float64)
    lg = np.log(1.0 - 2.0 ** (-5.0 - hh))
    n = np.arange(c, dtype=np.float64)
    rel = n[:, None] - n[None, :]
    dmat = np.where(rel[None] >= 0, np.exp(np.maximum(rel, 0.0)[None] * lg[:, None, None]), 0.0)
    rep = lambda a: np.repeat(a, HEAD_DIM, axis=-1)
    qdec = rep(np.exp((n + 1.0)[:, None] * lg[None, :]))
    kdec = rep(np.exp((c - 1.0 - n)[:, None] * lg[None, :]))
    gc = rep(np.exp(c * lg)[None, :])
    return dmat, qdec, kdec, gc


def _ret(x, ng, state, *, batch, seq, pos0):
    w = RET_WIDTH
    n = batch * seq
    c = math.gcd(seq, CHUNK)
    dmat, qdec, kdec, gc = _ret_decay_tables(c)
    cos, sin = _rope_tables(pos0, seq)
    f = lambda a: jnp.asarray(a, dtype=F32)
    if state is None:
        r = PROMPT_BLOCK
        nc = r // c
        nt = seq // r
        rows = lambda b, t: (b * nt + t, 0)
        trow = lambda b, t: (t, 0)
        const = lambda b, t: (0, 0)
        const3 = lambda b, t: (0, 0, 0)
        st = lambda b, t: (b, 0, 0, 0)
        out, ns = pl.pallas_call(
            functools.partial(_ret_prompt_kernel, r=r, c=c),
            grid=(batch, nt),
            in_specs=[pl.BlockSpec((r, RET_COLS), rows),
                      pl.BlockSpec((r, w), trow),
                      pl.BlockSpec((r, w), trow),
                      pl.BlockSpec((nc * N_HEADS, c, c), const3),
                      pl.BlockSpec((r, w), const),
                      pl.BlockSpec((r, w), const),
                      pl.BlockSpec((1, w), const),
                      pl.BlockSpec((1, w), const)],
            out_specs=[pl.BlockSpec((r, w), rows),
                       pl.BlockSpec((1, N_HEADS, HEAD_DIM, HEAD_DIM), st)],
            out_shape=[jax.ShapeDtypeStruct((n, w), F32),
                       jax.ShapeDtypeStruct((batch, N_HEADS, HEAD_DIM, HEAD_DIM), F32)],
            scratch_shapes=[pltpu.VMEM((N_HEADS, HEAD_DIM, HEAD_DIM), F32)],
            compiler_params=_cparams(("parallel", "arbitrary")),
            name="ret_prompt",
        )(x, cos, sin, f(np.tile(dmat, (nc, 1, 1))), f(np.tile(qdec, (nc, 1))), f(np.tile(kdec, (nc, 1))),
          f(gc), ng)
        return out, ns
    nseq = SAMPLE_SEQS
    r = nseq * c
    seg = np.arange(r) // c
    same = (seg[:, None] == seg[None, :])[None]
    dblk = np.where(same, np.tile(dmat, (1, nseq, nseq)), 0.0)
    rows = lambda i: (i, 0)
    const = lambda i: (0, 0)
    const3 = lambda i: (0, 0, 0)
    st = lambda i: (i, 0, 0, 0)
    sblk = (nseq, N_HEADS, HEAD_DIM, HEAD_DIM)
    out, ns = pl.pallas_call(
        functools.partial(_ret_sample_kernel, nseq=nseq, c=c),
        grid=(batch // nseq,),
        in_specs=[pl.BlockSpec((r, RET_COLS), rows),
                  pl.BlockSpec((r, w), const),
                  pl.BlockSpec((r, w), const),
                  pl.BlockSpec((N_HEADS, r, r), const3),
                  pl.BlockSpec((r, w), const),
                  pl.BlockSpec((r, w), const),
                  pl.BlockSpec((1, w), const),
                  pl.BlockSpec((1, w), const),
                  pl.BlockSpec(sblk, st)],
        out_specs=[pl.BlockSpec((r, w), rows), pl.BlockSpec(sblk, st)],
        out_shape=[jax.ShapeDtypeStruct((n, w), F32),
                   jax.ShapeDtypeStruct((batch, N_HEADS, HEAD_DIM, HEAD_DIM), F32)],
        compiler_params=_cparams(("parallel",)),
        name="ret_sample",
    )(x, jnp.tile(cos, (nseq, 1)), jnp.tile(sin, (nseq, 1)), f(dblk), f(np.tile(qdec, (nseq, 1))),
      f(np.tile(kdec, (nseq, 1))), f(gc), ng, state)
    return out, ns


def _unit_lower_inverse(a, eye, c):
    inv = eye - a
    x = a
    cover = 2
    while cover < c // 2:
        xm = x.astype(BF16)
        x = _bmm('bij,bjk->bik', xm, xm)
        inv = inv + _bmm('bij,bjk->bik', inv.astype(BF16), x.astype(BF16))
        cover *= 2
    resid = eye - inv - _bmm3('bij,bjk->bik', a, inv)
    return inv + _bmm('bij,bjk->bik', inv.astype(BF16), resid.astype(BF16))


def _gdn_gates(ab, alog_ref, dtb_ref, tri_l, tri_u):
    g_log = -jnp.exp(alog_ref[...]) * _softplus(ab + dtb_ref[...])
    beta = jax.nn.sigmoid(ab)
    hi = lax.Precision.HIGHEST
    g_col = _dot(tri_l.astype(F32), g_log, precision=hi)
    g_row = _dot_tn(g_log, tri_u.astype(F32), precision=hi)
    return g_col, g_row, beta


def _l2norm_heads(x, base):
    out = []
    for h in range(N_HEADS):
        xr = x[:, _head_slice(h, base)]
        out.append(xr * lax.rsqrt(jnp.sum(xr * xr, axis=-1, keepdims=True) + EPS))
    return out


def _gdn_prompt_kernel(x_ref, cw_ref, alog_ref, dtb_ref, ng_ref, out_ref, nconv_ref, ns_ref, ext, s_scr,
                       *, r, c):
    t = pl.program_id(1)
    last = pl.num_programs(1) - 1
    w = GDN_WIDTH
    hist = SUBLANES - (CONV_W - 1)
    nc = r // c

    @pl.when(t == 0)
    def _():
        ext[0:SUBLANES, :] = jnp.zeros((SUBLANES, 3 * w), F32)
        s_scr[...] = jnp.zeros(s_scr.shape, F32)

    qkv = _silu(_conv_unit(ext, x_ref[:, 0:3 * w], cw_ref, r))

    @pl.when(t == last)
    def _():
        nconv_ref[0] = ext[r + hist:r + SUBLANES, :]

    ext[0:SUBLANES, :] = ext[r:r + SUBLANES, :]
    tri_l, _, tri_u = _segment_masks(r, c)
    g_col, g_row, beta = _gdn_gates(x_ref[:, 4 * w:4 * w + LANES], alog_ref, dtb_ref, tri_l, tri_u)
    qn = [a * (HEAD_DIM ** -0.5) for a in _l2norm_heads(qkv, 0)]
    kn = _l2norm_heads(qkv, w)
    rows = lambda n: slice(n * c, (n + 1) * c)

    def stack(fn):
        return jnp.stack([fn(n, h) for n in range(nc) for h in range(N_HEADS)], axis=0)

    q = stack(lambda n, h: qn[h][rows(n)])
    k = stack(lambda n, h: kn[h][rows(n)])
    v = stack(lambda n, h: qkv[rows(n), _head_slice(h, 2 * w)])
    gi = stack(lambda n, h: g_col[rows(n), h:h + 1])
    gj = stack(lambda n, h: g_row[h:h + 1, rows(n)])
    bi = stack(lambda n, h: beta[rows(n), GDN_BETA_LANE + h:GDN_BETA_LANE + h + 1])
    gl = stack(lambda n, h: g_col[(n + 1) * c - 1:(n + 1) * c, h:h + 1])
    lower, strict, _ = _segment_masks(c, c)
    eye = (lower & jnp.logical_not(strict)).astype(F32)[None]
    decay = jnp.exp(jnp.where(lower[None], gi - gj, -jnp.inf))
    kb = k.astype(BF16)
    a_mat = jnp.where(strict[None], bi * _bmm('bid,bjd->bij', kb, kb) * decay, 0.0)
    tm = _unit_lower_inverse(a_mat, eye, c)
    eg = jnp.exp(gi)
    rhs = jnp.concatenate([bi * v, (bi * eg) * k], axis=2)
    uw = _bmm('bij,bjk->bik', tm.astype(BF16), rhs.astype(BF16))
    uu = uw[:, :, 0:HEAD_DIM]
    pm = (_bmm('bid,bjd->bij', q.astype(BF16), kb) * decay).astype(BF16)
    wq = jnp.concatenate([uw[:, :, HEAD_DIM:2 * HEAD_DIM], q * eg], axis=1).astype(BF16)
    kd = (k * jnp.exp(gl - gi)).astype(BF16)
    egl = jnp.exp(gl)
    z = x_ref[:, 3 * w:4 * w]
    s = s_scr[...]
    for n in range(nc):
        ps = slice(n * N_HEADS, (n + 1) * N_HEADS)
        wqs = _bmm('hid,hde->hie', wq[ps], s.astype(BF16))
        vn = (uu[ps] - wqs[:, 0:c]).astype(BF16)
        o = wqs[:, c:2 * c] + _bmm('hij,hje->hie', pm[ps], vn)
        s = egl[ps] * s + _bmm('hid,hie->hde', kd[ps], vn)
        on = _head_rms(o) * ng_ref[...]
        for h in range(N_HEADS):
            hs = _head_slice(h)
            out_ref[rows(n), hs] = on[h] * _silu(z[rows(n), hs])
    s_scr[...] = s

    @pl.when(t == last)
    def _():
        ns_ref[0] = s


def _gdn_sample_kernel(x_ref, cw_ref, alog_ref, dtb_ref, ng_ref, buf_ref, s0_ref,
                       out_ref, nconv_ref, ns_ref, ext, y_scr, *, nseq, c):
    w = GDN_WIDTH
    hist = SUBLANES - (CONV_W - 1)
    r = nseq * c

    def conv_body(u, carry_):
        r0 = pl.multiple_of(u * c, SUBLANES)
        ext[hist:SUBLANES, :] = buf_ref[u]
        y_scr[pl.ds(r0, c), :] = _conv_unit(ext, x_ref[pl.ds(r0, c), 0:3 * w], cw_ref, c)
        nconv_ref[u] = ext[c + hist:c + SUBLANES, :]
        return carry_

    lax.fori_loop(0, nseq, conv_body, 0)
    qkv = _silu(y_scr[...])
    lower, strict, upper = _segment_masks(r, c)
    g_col, g_row, beta = _gdn_gates(x_ref[:, 4 * w:4 * w + LANES], alog_ref, dtb_ref, lower, upper)
    g_seq = g_col.reshape(nseq, c, LANES)
    g_end = g_seq[:, c - 1:c, :]
    g_end_rows = jnp.broadcast_to(g_end, (nseq, c, LANES)).reshape(r, LANES)
    qn = [a * (HEAD_DIM ** -0.5) for a in _l2norm_heads(qkv, 0)]
    kn = _l2norm_heads(qkv, w)
    heads = range(N_HEADS)
    hstack = lambda fn: jnp.stack([fn(h) for h in heads], axis=0)
    q = hstack(lambda h: qn[h])
    k = hstack(lambda h: kn[h])
    v = hstack(lambda h: qkv[:, _head_slice(h, 2 * w)])
    gi = hstack(lambda h: g_col[:, h:h + 1])
    gj = hstack(lambda h: g_row[h:h + 1, :])
    bi = hstack(lambda h: beta[:, GDN_BETA_LANE + h:GDN_BETA_LANE + h + 1])
    ge = hstack(lambda h: g_end_rows[:, h:h + 1])
    eye = (lower & jnp.logical_not(strict)).astype(F32)[None]
    decay = jnp.exp(jnp.where(lower[None], gi - gj, -jnp.inf))
    kb = k.astype(BF16)
    a_mat = jnp.where(strict[None], bi * _bmm('bid,bjd->bij', kb, kb) * decay, 0.0)
    tm = _unit_lower_inverse(a_mat, eye, c)
    eg = jnp.exp(gi)
    rhs = jnp.concatenate([bi * v, (bi * eg) * k], axis=2)
    uw = _bmm('bij,bjk->bik', tm.astype(BF16), rhs.astype(BF16))
    pm = (_bmm('bid,bjd->bij', q.astype(BF16), kb) * decay).astype(BF16)
    qg = q * eg
    kd = k * jnp.exp(ge - gi)
    per_seq = lambda a: a.reshape(nseq, c, HEAD_DIM)
    s0 = [s0_ref[:, h] for h in heads]
    wqs = [_bmm('bid,bde->bie',
                jnp.concatenate([per_seq(uw[h][:, HEAD_DIM:2 * HEAD_DIM]), per_seq(qg[h])], axis=1), s0[h])
           for h in heads]
    vn = [per_seq(uw[h][:, 0:HEAD_DIM]) - wqs[h][:, 0:c] for h in heads]
    o = [wqs[h][:, c:2 * c].reshape(r, HEAD_DIM)
         + _dot(pm[h], vn[h].reshape(r, HEAD_DIM).astype(BF16)) for h in heads]
    upd = [_bmm('bid,bie->bde', per_seq(kd[h]), vn[h]) for h in heads]
    z = x_ref[:, 3 * w:4 * w]
    for h in heads:
        hs = _head_slice(h)
        ns_ref[:, h] = jnp.exp(g_end[:, :, h:h + 1]) * s0[h] + upd[h]
        out_ref[:, hs] = _head_rms(o[h]) * ng_ref[...] * _silu(z[:, hs])


def _gdn(x, cw, alog, dtb, ng, state, *, batch, seq):
    w = GDN_WIDTH
    n = batch * seq
    c = math.gcd(seq, CHUNK)
    if state is None:
        r = PROMPT_BLOCK
        nt = seq // r
        rows = lambda b, t: (b * nt + t, 0)
        const = lambda b, t: (0, 0)
        st3 = lambda b, t: (b, 0, 0)
        st4 = lambda b, t: (b, 0, 0, 0)
        return pl.pallas_call(
            functools.partial(_gdn_prompt_kernel, r=r, c=c),
            grid=(batch, nt),
            in_specs=[pl.BlockSpec((r, GDN_COLS), rows),
                      pl.BlockSpec((CONV_W, 3 * w), const),
                      pl.BlockSpec((1, LANES), const),
                      pl.BlockSpec((1, LANES), const),
                      pl.BlockSpec((1, HEAD_DIM), const)],
            out_specs=[pl.BlockSpec((r, w), rows),
                       pl.BlockSpec((1, CONV_W - 1, 3 * w), st3),
                       pl.BlockSpec((1, N_HEADS, HEAD_DIM, HEAD_DIM), st4)],
            out_shape=[jax.ShapeDtypeStruct((n, w), F32),
                       jax.ShapeDtypeStruct((batch, CONV_W - 1, 3 * w), F32),
                       jax.ShapeDtypeStruct((batch, N_HEADS, HEAD_DIM, HEAD_DIM), F32)],
            scratch_shapes=[pltpu.VMEM((r + SUBLANES, 3 * w), F32),
                            pltpu.VMEM((N_HEADS, HEAD_DIM, HEAD_DIM), F32)],
            compiler_params=_cparams(("parallel", "arbitrary")),
            name="gdn_prompt",
        )(x, cw, alog, dtb, ng)
    nseq = SAMPLE_SEQS
    r = nseq * c
    rows = lambda i: (i, 0)
    const = lambda i: (0, 0)
    st3 = lambda i: (i, 0, 0)
    st4 = lambda i: (i, 0, 0, 0)
    cblk = (nseq, CONV_W - 1, 3 * w)
    sblk = (nseq, N_HEADS, HEAD_DIM, HEAD_DIM)
    return pl.pallas_call(
        functools.partial(_gdn_sample_kernel, nseq=nseq, c=c),
        grid=(batch // nseq,),
        in_specs=[pl.BlockSpec((r, GDN_COLS), rows),
                  pl.BlockSpec((CONV_W, 3 * w), const),
                  pl.BlockSpec((1, LANES), const),
                  pl.BlockSpec((1, LANES), const),
                  pl.BlockSpec((1, HEAD_DIM), const),
                  pl.BlockSpec(cblk, st3),
                  pl.BlockSpec(sblk, st4)],
        out_specs=[pl.BlockSpec((r, w), rows), pl.BlockSpec(cblk, st3), pl.BlockSpec(sblk, st4)],
        out_shape=[jax.ShapeDtypeStruct((n, w), F32),
                   jax.ShapeDtypeStruct((batch, CONV_W - 1, 3 * w), F32),
                   jax.ShapeDtypeStruct((batch, N_HEADS, HEAD_DIM, HEAD_DIM), F32)],
        scratch_shapes=[pltpu.VMEM((c + SUBLANES, 3 * w), F32), pltpu.VMEM((r, 3 * w), F32)],
        compiler_params=_cparams(("parallel",)),
        name="gdn_sample",
    )(x, cw, alog, dtb, ng, *state)


def _block_diag(wb):
    out = jnp.zeros((LRU_WIDTH, LRU_WIDTH), wb.dtype)
    for n in range(LRU_BLOCKS):
        s = slice(n * LRU_BLOCK, (n + 1) * LRU_BLOCK)
        out = out.at[s, s].set(wb[n])
    return out


def _pad_lanes(v, offset=0):
    return jnp.zeros((1, LANES), F32).at[0, offset:offset + v.shape[0]].set(v.astype(F32))


def _group_cols(w):
    w = w.reshape(N_GROUPS, EXPERTS_PER_GROUP, D_MODEL, D_EXPERT)
    return jnp.transpose(w, (0, 2, 1, 3)).reshape(N_GROUPS, D_MODEL, D_GROUP).astype(BF16)


def _prep_layer(W, l):
    row = lambda v: v.reshape(1, -1).astype(F32)
    pad = IN_COLS - W['w_in'].shape[2]
    wr = jnp.zeros((D_MODEL, LANES), F32)
    wr = wr.at[:, ROUTER_GROUP_LANE:ROUTER_GROUP_LANE + N_GROUPS].set(W['w_router_group'][l])
    wr = wr.at[:, ROUTER_EXPERT_LANE:ROUTER_EXPERT_LANE + N_EXPERTS].set(W['w_router_expert'][l])
    br = jnp.zeros((1, LANES), F32)
    br = br.at[0, ROUTER_GROUP_LANE:ROUTER_GROUP_LANE + N_GROUPS].set(W['b_router_group'][l])
    br = br.at[0, ROUTER_EXPERT_LANE:ROUTER_EXPERT_LANE + N_EXPERTS].set(W['b_router_expert'][l])
    return dict(
        norm_mix_g=row(W['norm_mix_g'][l]),
        w_in=jnp.pad(W['w_in'][l], ((0, 0), (0, pad))).astype(BF16),
        lru_conv_w=W['lru_conv_w'][l], lru_conv_b=row(W['lru_conv_b'][l]),
        lru_wa=_block_diag(W['lru_wa'][l]).astype(BF16), lru_ba=row(W['lru_ba'][l]),
        lru_wx=_block_diag(W['lru_wx'][l]).astype(BF16), lru_bx=row(W['lru_bx'][l]),
        lru_lambda=row(W['lru_lambda'][l]),
        ret_norm_g=row(W['ret_norm_g'][l]),
        gdn_conv_w=W['gdn_conv_w'][l],
        gdn_a_log=_pad_lanes(W['gdn_a_log'][l]), gdn_dt_bias=_pad_lanes(W['gdn_dt_bias'][l]),
        gdn_norm_g=row(W['gdn_norm_g'][l]),
        w_out=W['w_out'][l].astype(BF16),
        norm_ffn_g=row(W['norm_ffn_g'][l]),
        w_router=wr.astype(BF16), b_router=br,
        w_expert_gate=_group_cols(W['w_expert_gate'][l]),
        w_expert_up=_group_cols(W['w_expert_up'][l]),
        w_expert_down=W['w_expert_down'][l].reshape(N_GROUPS, D_GROUP, D_MODEL).astype(BF16),
        norm_pe_g=row(W['norm_pe_g'][l]),
        w_pe=W['w_pe'][l].astype(BF16),
        w_pe_gate=W['w_pe_gate'][l].astype(BF16),
    )


def _trunk(x, p, states, pos0, layers, final_g):
    batch, seq, _ = x.shape
    n = batch * seq
    h = x.reshape(n, D_MODEL)
    new = ([], [], [], [], [])
    for l, L in enumerate(layers):
        if states is None:
            st_lru = st_ret = st_gdn = None
        else:
            st_lru = (states[0][l], states[1][l])
            st_ret = states[2][l]
            st_gdn = (states[3][l], states[4][l])
        lru_x, ret_x, gdn_x = _in_proj(h, L['norm_mix_g'], L['w_in'])
        out_a, n_lru_conv, n_lru_h = _lru(
            lru_x, L['lru_conv_w'], L['lru_conv_b'], L['lru_wa'], L['lru_ba'], L['lru_wx'],
            L['lru_bx'], L['lru_lambda'], st_lru, batch=batch, seq=seq)
        out_b, n_ret = _ret(ret_x, L['ret_norm_g'], st_ret, batch=batch, seq=seq, pos0=pos0)
        out_c, n_gdn_conv, n_gdn = _gdn(
            gdn_x, L['gdn_conv_w'], L['gdn_a_log'], L['gdn_dt_bias'], L['gdn_norm_g'], st_gdn,
            batch=batch, seq=seq)
        h1, xn, gates = _out_router(h, out_a, out_b, out_c, L['w_out'], L['norm_ffn_g'],
                                    L['w_router'], L['b_router'])
        h2 = _moe(xn, gates, h1, L['w_expert_gate'], L['w_expert_up'], L['w_expert_down'])
        h = _pe(h2, p[l].reshape(n, PE_DIM), L['norm_pe_g'], L['w_pe_gate'], L['w_pe'], final_g,
                final=(l == len(layers) - 1))
        for lst, s in zip(new, (n_lru_conv, n_lru_h, n_ret, n_gdn_conv, n_gdn)):
            lst.append(s)
    return h.reshape(batch, seq, D_MODEL), tuple(jnp.stack(lst) for lst in new)


def kernel(x_prompt, x_sample, p_prompt, p_sample, state_lru_conv, state_lru_h, state_ret, state_gdn_conv, state_gdn, norm_mix_g, w_in, lru_conv_w, lru_conv_b, lru_wa, lru_ba, lru_wx, lru_bx, lru_lambda, ret_norm_g, gdn_conv_w, gdn_a_log, gdn_dt_bias, gdn_norm_g, w_out, norm_ffn_g, w_router_group, b_router_group, w_router_expert, b_router_expert, w_expert_gate, w_expert_up, w_expert_down, norm_pe_g, w_pe, w_pe_gate, final_norm_g):
    W = dict(norm_mix_g=norm_mix_g, w_in=w_in, lru_conv_w=lru_conv_w, lru_conv_b=lru_conv_b,
             lru_wa=lru_wa, lru_ba=lru_ba, lru_wx=lru_wx, lru_bx=lru_bx, lru_lambda=lru_lambda,
             ret_norm_g=ret_norm_g, gdn_conv_w=gdn_conv_w, gdn_a_log=gdn_a_log, gdn_dt_bias=gdn_dt_bias,
             gdn_norm_g=gdn_norm_g, w_out=w_out, norm_ffn_g=norm_ffn_g, w_router_group=w_router_group,
             b_router_group=b_router_group, w_router_expert=w_router_expert, b_router_expert=b_router_expert,
             w_expert_gate=w_expert_gate, w_expert_up=w_expert_up, w_expert_down=w_expert_down,
             norm_pe_g=norm_pe_g, w_pe=w_pe, w_pe_gate=w_pe_gate)
    layers = [_prep_layer(W, l) for l in range(DEPTH)]
    final_g = final_norm_g.reshape(1, D_MODEL).astype(F32)
    y_p, st_p = _trunk(x_prompt, p_prompt, None, 0, layers, final_g)
    sample_states = (state_lru_conv, state_lru_h, state_ret, state_gdn_conv, state_gdn)
    y_s, st_s = _trunk(x_sample, p_sample, sample_states, PAST_LEN, layers, final_g)
    return (y_p, y_s) + st_p + st_s
```

```python
import functools
import math

import jax
import jax.numpy as jnp
import numpy as np
from jax import lax
from jax.experimental import pallas as pl
from jax.experimental.pallas import tpu as pltpu

F32 = jnp.float32
BF16 = jnp.bfloat16

D_MODEL = 1024
DEPTH = 2
PAST_LEN = 16384
HEAD_DIM = 64
HALF = HEAD_DIM // 2
LRU_WIDTH = 256
LRU_BLOCKS = 4
LRU_BLOCK = 64
LRU_C = 8.0
RET_WIDTH = 384
GDN_WIDTH = 384
N_HEADS = 6
CONV_W = 4
CHUNK = 64
ROPE_BASE = 10000.0
PE_DIM = 256
N_GROUPS = 4
EXPERTS_PER_GROUP = 4
N_EXPERTS = 16
D_EXPERT = 256
EPS = 1e-6

LANES = 128
SUBLANES = 8
VMEM_LIMIT = 48 * 1024 * 1024

LRU_COLS = 2 * LRU_WIDTH
RET_COLS = 4 * RET_WIDTH
GDN_COLS = 4 * GDN_WIDTH + LANES
IN_COLS = LRU_COLS + RET_COLS + GDN_COLS
ROUTER_GROUP_LANE = 0
ROUTER_EXPERT_LANE = N_GROUPS
GDN_BETA_LANE = N_HEADS

D_GROUP = EXPERTS_PER_GROUP * D_EXPERT

TOKEN_TILE = 512
MOE_TILE = 1024
PROMPT_BLOCK = 256
SAMPLE_SEQS = 16


def _cparams(sem):
    return pltpu.CompilerParams(dimension_semantics=sem, vmem_limit_bytes=VMEM_LIMIT)


def _rmsnorm(x, g):
    return x * lax.rsqrt(jnp.mean(x * x, axis=-1, keepdims=True) + EPS) * g


def _dot(a, b, **kw):
    return jnp.dot(a, b, preferred_element_type=F32, **kw)


def _dot_nt(a, b):
    return lax.dot_general(a, b, (((1,), (1,)), ((), ())), preferred_element_type=F32)


def _dot_tn(a, b, **kw):
    return lax.dot_general(a, b, (((0,), (0,)), ((), ())), preferred_element_type=F32, **kw)


def _bmm(spec, a, b):
    return jnp.einsum(spec, a, b, preferred_element_type=F32)


def _bmm3(spec, a, b):
    a_hi = a.astype(BF16)
    a_lo = (a - a_hi.astype(F32)).astype(BF16)
    b_hi = b.astype(BF16)
    b_lo = (b - b_hi.astype(F32)).astype(BF16)
    return _bmm(spec, a_hi, b_hi) + (_bmm(spec, a_hi, b_lo) + _bmm(spec, a_lo, b_hi))


def _softplus(x):
    return jnp.maximum(x, 0.0) + jnp.log1p(jnp.exp(-jnp.abs(x)))


def _silu(x):
    return x * jax.nn.sigmoid(x)


def _gelu_tanh(x):
    return 0.5 * x * (1.0 + jnp.tanh(math.sqrt(2.0 / math.pi) * (x + 0.044715 * (x * x * x))))


def _in_proj_kernel(h_ref, g_ref, w_ref, lru_ref, ret_ref, gdn_ref):
    xn = _rmsnorm(h_ref[...], g_ref[...]).astype(BF16)
    lru_ref[...] = _dot(xn, w_ref[:, 0:LRU_COLS])
    ret_ref[...] = _dot(xn, w_ref[:, LRU_COLS:LRU_COLS + RET_COLS])
    gdn_ref[...] = _dot(xn, w_ref[:, LRU_COLS + RET_COLS:IN_COLS])


def _in_proj(h, g, w):
    n = h.shape[0]
    tm = TOKEN_TILE
    row = lambda i: (i, 0)
    const = lambda i: (0, 0)
    return pl.pallas_call(
        _in_proj_kernel,
        grid=(n // tm,),
        in_specs=[pl.BlockSpec((tm, D_MODEL), row),
                  pl.BlockSpec((1, D_MODEL), const),
                  pl.BlockSpec((D_MODEL, IN_COLS), const)],
        out_specs=[pl.BlockSpec((tm, LRU_COLS), row),
                   pl.BlockSpec((tm, RET_COLS), row),
                   pl.BlockSpec((tm, GDN_COLS), row)],
        out_shape=[jax.ShapeDtypeStruct((n, LRU_COLS), F32),
                   jax.ShapeDtypeStruct((n, RET_COLS), F32),
                   jax.ShapeDtypeStruct((n, GDN_COLS), F32)],
        compiler_params=_cparams(("parallel",)),
        name="in_proj",
    )(h, g, w)


def _route(logits):
    lane = lax.broadcasted_iota(jnp.int32, logits.shape, 1).astype(F32)
    neg = -jnp.inf
    far = float(LANES)
    gmask = lane < N_GROUPS
    gl = jnp.where(gmask, logits, neg)
    gmax = jnp.max(gl, axis=-1, keepdims=True)
    gsel = jnp.min(jnp.where(gl == gmax, lane, far), axis=-1, keepdims=True)
    gsum = jnp.sum(jnp.where(gmask, jnp.exp(logits - gmax), 0.0), axis=-1, keepdims=True)
    g_w = 1.0 / gsum
    lo = ROUTER_EXPERT_LANE + EXPERTS_PER_GROUP * gsel
    el = jnp.where(lane >= lo, jnp.where(lane < lo + EXPERTS_PER_GROUP, logits, neg), neg)
    m1 = jnp.max(el, axis=-1, keepdims=True)
    i1 = jnp.min(jnp.where(el == m1, lane, far), axis=-1, keepdims=True)
    el2 = jnp.where(lane == i1, neg, el)
    m2 = jnp.max(el2, axis=-1, keepdims=True)
    i2 = jnp.min(jnp.where(el2 == m2, lane, far), axis=-1, keepdims=True)
    e2 = jnp.exp(m2 - m1)
    w1 = g_w / (1.0 + e2)
    w2 = w1 * e2
    return jnp.where(lane == i1, w1, 0.0) + jnp.where(lane == i2, w2, 0.0)


def _out_router_kernel(h_ref, a_ref, b_ref, c_ref, wo_ref, g_ref, wr_ref, br_ref,
                       h1_ref, xn_ref, gates_ref):
    o1 = LRU_WIDTH
    o2 = LRU_WIDTH + RET_WIDTH
    mix = (_dot(a_ref[...].astype(BF16), wo_ref[0:o1, :])
           + _dot(b_ref[...].astype(BF16), wo_ref[o1:o2, :])
           + _dot(c_ref[...].astype(BF16), wo_ref[o2:D_MODEL, :]))
    h1 = h_ref[...] + mix
    h1_ref[...] = h1
    xn = _rmsnorm(h1, g_ref[...]).astype(BF16)
    xn_ref[...] = xn
    gates_ref[...] = _route(_dot(xn, wr_ref[...]) + br_ref[...])


def _out_router(h, a, b, c, wo, g, wr, br):
    n = h.shape[0]
    tm = TOKEN_TILE
    row = lambda i: (i, 0)
    const = lambda i: (0, 0)
    return pl.pallas_call(
        _out_router_kernel,
        grid=(n // tm,),
        in_specs=[pl.BlockSpec((tm, D_MODEL), row),
                  pl.BlockSpec((tm, LRU_WIDTH), row),
                  pl.BlockSpec((tm, RET_WIDTH), row),
                  pl.BlockSpec((tm, GDN_WIDTH), row),
                  pl.BlockSpec((D_MODEL, D_MODEL), const),
                  pl.BlockSpec((1, D_MODEL), const),
                  pl.BlockSpec((D_MODEL, LANES), const),
                  pl.BlockSpec((1, LANES), const)],
        out_specs=[pl.BlockSpec((tm, D_MODEL), row),
                   pl.BlockSpec((tm, D_MODEL), row),
                   pl.BlockSpec((tm, LANES), row)],
        out_shape=[jax.ShapeDtypeStruct((n, D_MODEL), F32),
                   jax.ShapeDtypeStruct((n, D_MODEL), BF16),
                   jax.ShapeDtypeStruct((n, LANES), F32)],
        compiler_params=_cparams(("parallel",)),
        name="out_router",
    )(h, a, b, c, wo, g, wr, br)


def _moe_kernel(xn_ref, gates_ref, h1_ref, wg_ref, wu_ref, wd_ref, out_ref):
    grp = pl.program_id(1)

    @pl.when(grp == 0)
    def _():
        out_ref[...] = h1_ref[...]

    xn = xn_ref[...]
    hg = _dot(xn, wg_ref[...])
    hu = _dot(xn, wu_ref[...])
    gates = gates_ref[...]
    lane = lax.broadcasted_iota(jnp.int32, gates.shape, 1)
    base = ROUTER_EXPERT_LANE + EXPERTS_PER_GROUP * grp
    parts = []
    for j in range(EXPERTS_PER_GROUP):
        sl = slice(j * D_EXPERT, (j + 1) * D_EXPERT)
        gcol = jnp.sum(jnp.where(lane == base + j, gates, 0.0), axis=-1, keepdims=True)
        parts.append((_silu(hg[:, sl]) * hu[:, sl] * gcol).astype(BF16))
    out_ref[...] += _dot(jnp.concatenate(parts, axis=1), wd_ref[...])


def _moe(xn, gates, h1, wg, wu, wd):
    n = xn.shape[0]
    tm = MOE_TILE if n % MOE_TILE == 0 else TOKEN_TILE
    row = lambda i, e: (i, 0)
    return pl.pallas_call(
        _moe_kernel,
        grid=(n // tm, N_GROUPS),
        in_specs=[pl.BlockSpec((tm, D_MODEL), row),
                  pl.BlockSpec((tm, LANES), row),
                  pl.BlockSpec((tm, D_MODEL), row),
                  pl.BlockSpec((None, D_MODEL, D_GROUP), lambda i, e: (e, 0, 0)),
                  pl.BlockSpec((None, D_MODEL, D_GROUP), lambda i, e: (e, 0, 0)),
                  pl.BlockSpec((None, D_GROUP, D_MODEL), lambda i, e: (e, 0, 0))],
        out_specs=pl.BlockSpec((tm, D_MODEL), row),
        out_shape=jax.ShapeDtypeStruct((n, D_MODEL), F32),
        compiler_params=_cparams(("parallel", "arbitrary")),
        name="moe",
    )(xn, gates, h1, wg, wu, wd)


def _pe_kernel(h_ref, p_ref, g_ref, wgate_ref, wpe_ref, fg_ref, out_ref, *, final):
    h = h_ref[...]
    xn = _rmsnorm(h, g_ref[...]).astype(BF16)
    gate = jax.nn.sigmoid(_dot(xn, wgate_ref[...]))
    pe = _dot(p_ref[...].astype(BF16), wpe_ref[...])
    h3 = h + pe * gate
    out_ref[...] = _rmsnorm(h3, fg_ref[...]) if final else h3


def _pe(h, p, g, wgate, wpe, fg, final):
    n = h.shape[0]
    tm = TOKEN_TILE
    row = lambda i: (i, 0)
    const = lambda i: (0, 0)
    return pl.pallas_call(
        functools.partial(_pe_kernel, final=final),
        grid=(n // tm,),
        in_specs=[pl.BlockSpec((tm, D_MODEL), row),
                  pl.BlockSpec((tm, PE_DIM), row),
                  pl.BlockSpec((1, D_MODEL), const),
                  pl.BlockSpec((D_MODEL, D_MODEL), const),
                  pl.BlockSpec((PE_DIM, D_MODEL), const),
                  pl.BlockSpec((1, D_MODEL), const)],
        out_specs=pl.BlockSpec((tm, D_MODEL), row),
        out_shape=jax.ShapeDtypeStruct((n, D_MODEL), F32),
        compiler_params=_cparams(("parallel",)),
        name="pe_gate",
    )(h, p, g, wgate, wpe, fg)


def _conv_unit(ext, x, w_ref, c):
    ext[SUBLANES:SUBLANES + c, :] = x
    y = w_ref[CONV_W - 1:CONV_W, :] * x
    for j in range(CONV_W - 1):
        off = SUBLANES - (CONV_W - 1) + j
        y = y + w_ref[j:j + 1, :] * ext[off:off + c, :]
    return y


def _lru_kernel(*refs, c, units, carry):
    if carry:
        (x_ref, cw_ref, cb_ref, wa_ref, ba_ref, wx_ref, bx_ref, lam_ref,
         out_ref, nconv_ref, nh_ref, ext, y_scr, a_scr, b_scr, hcar) = refs
    else:
        (x_ref, cw_ref, cb_ref, wa_ref, ba_ref, wx_ref, bx_ref, lam_ref, buf_ref, h0_ref,
         out_ref, nconv_ref, nh_ref, ext, y_scr, a_scr, b_scr) = refs
    w = LRU_WIDTH
    hist = SUBLANES - (CONV_W - 1)
    if carry:
        t = pl.program_id(1)
        last = pl.num_programs(1) - 1

        @pl.when(t == 0)
        def _():
            ext[0:SUBLANES, :] = jnp.zeros((SUBLANES, w), F32)
            hcar[...] = jnp.zeros((1, w), F32)

    def conv_body(u, carry_):
        r0 = pl.multiple_of(u * c, SUBLANES)
        if not carry:
            ext[hist:SUBLANES, :] = buf_ref[u]
        x = x_ref[pl.ds(r0, c), 0:w]
        y_scr[pl.ds(r0, c), :] = _conv_unit(ext, x, cw_ref, c) + cb_ref[...]
        if carry:
            @pl.when(t == last)
            def _():
                nconv_ref[0] = ext[c + hist:c + SUBLANES, :]
            ext[0:SUBLANES, :] = ext[c:c + SUBLANES, :]
        else:
            nconv_ref[u] = ext[c + hist:c + SUBLANES, :]
        return carry_

    lax.fori_loop(0, units, conv_body, 0)

    y = y_scr[...]
    yb = y.astype(BF16)
    r = jax.nn.sigmoid(_dot(yb, wa_ref[...]) + ba_ref[...])
    i = jax.nn.sigmoid(_dot(yb, wx_ref[...]) + bx_ref[...])
    log_a = (-LRU_C) * r * _softplus(-lam_ref[...])
    th = jnp.tanh(log_a)
    a_scr[...] = jnp.exp(log_a)
    b_scr[...] = jnp.sqrt(-2.0 * th / (1.0 - th)) * (i * y)

    def scan_body(u, carry_):
        r0 = pl.multiple_of(u * c, SUBLANES)
        a = a_scr[pl.ds(r0, c), :]
        b = b_scr[pl.ds(r0, c), :]
        rowi = lax.broadcasted_iota(jnp.int32, (c, w), 0)
        d = 1
        while d < c:
            keep = rowi >= d
            a_s = pltpu.roll(a, d, axis=0)
            b_s = pltpu.roll(b, d, axis=0)
            b = jnp.where(keep, a * b_s + b, b)
            a = jnp.where(keep, a * a_s, a)
            d *= 2
        h0 = hcar[...] if carry else h0_ref[u]
        h = a * h0 + b
        gate = x_ref[pl.ds(r0, c), w:2 * w]
        out_ref[pl.ds(r0, c), :] = h * _gelu_tanh(gate)
        if carry:
            hcar[...] = h[c - 1:c, :]
        else:
            nh_ref[u] = h[c - 1:c, :]
        return carry_

    lax.fori_loop(0, units, scan_body, 0)
    if carry:
        @pl.when(t == last)
        def _():
            nh_ref[0] = hcar[...]


def _lru(x, cw, cb, wa, ba, wx, bx, lam, state, *, batch, seq):
    w = LRU_WIDTH
    carry = state is None
    if carry:
        c, units = PROMPT_BLOCK, 1
        nt = seq // c
        grid = (batch, nt)
        rows = lambda b, t: (b * nt + t, 0)
        const = lambda b, t: (0, 0)
        st = lambda b, t: (b, 0, 0)
        sem = ("parallel", "arbitrary")
        state_in, state_specs = [], []
    else:
        c, units = seq, batch
        grid = (1,)
        rows = lambda i: (0, 0)
        const = lambda i: (0, 0)
        st = lambda i: (0, 0, 0)
        sem = ("arbitrary",)
        state_in = [state[0], state[1].reshape(batch, 1, w)]
        state_specs = [pl.BlockSpec((units, CONV_W - 1, w), st), pl.BlockSpec((units, 1, w), st)]
    n = batch * seq
    blk = c * units
    scratch = [pltpu.VMEM((c + SUBLANES, w), F32), pltpu.VMEM((blk, w), F32),
               pltpu.VMEM((blk, w), F32), pltpu.VMEM((blk, w), F32)]
    if carry:
        scratch.append(pltpu.VMEM((1, w), F32))
    out, nconv, nh = pl.pallas_call(
        functools.partial(_lru_kernel, c=c, units=units, carry=carry),
        grid=grid,
        in_specs=[pl.BlockSpec((blk, LRU_COLS), rows),
                  pl.BlockSpec((CONV_W, w), const),
                  pl.BlockSpec((1, w), const),
                  pl.BlockSpec((w, w), const),
                  pl.BlockSpec((1, w), const),
                  pl.BlockSpec((w, w), const),
                  pl.BlockSpec((1, w), const),
                  pl.BlockSpec((1, w), const)] + state_specs,
        out_specs=[pl.BlockSpec((blk, w), rows),
                   pl.BlockSpec((units, CONV_W - 1, w), st),
                   pl.BlockSpec((units, 1, w), st)],
        out_shape=[jax.ShapeDtypeStruct((n, w), F32),
                   jax.ShapeDtypeStruct((batch, CONV_W - 1, w), F32),
                   jax.ShapeDtypeStruct((batch, 1, w), F32)],
        scratch_shapes=scratch,
        compiler_params=_cparams(sem),
        name="lru_prompt" if carry else "lru_sample",
    )(x, cw, cb, wa, ba, wx, bx, lam, *state_in)
    return out, nconv, nh.reshape(batch, w)


def _rope(x, cos, sin):
    parts = []
    for j in range(x.shape[1] // LANES):
        sl = slice(j * LANES, (j + 1) * LANES)
        xs = x[:, sl]
        lane = lax.broadcasted_iota(jnp.int32, xs.shape, 1)
        swapped = jnp.where((lane & HALF) == 0,
                            pltpu.roll(xs, LANES - HALF, axis=1), pltpu.roll(xs, HALF, axis=1))
        parts.append(xs * cos[:, sl] + swapped * sin[:, sl])
    return jnp.concatenate(parts, axis=1)


def _segment_masks(r, c):
    ri = lax.broadcasted_iota(jnp.int32, (r, r), 0)
    ci = lax.broadcasted_iota(jnp.int32, (r, r), 1)
    if c == r:
        return ri >= ci, ri > ci, ri <= ci
    shift = c.bit_length() - 1
    same = lax.shift_right_logical(ri, shift) == lax.shift_right_logical(ci, shift)
    return same & (ri >= ci), same & (ri > ci), same & (ri <= ci)


def _head_slice(h, base=0):
    return slice(base + h * HEAD_DIM, base + (h + 1) * HEAD_DIM)


def _head_rms(o):
    return o * lax.rsqrt(jnp.mean(o * o, axis=-1, keepdims=True) + EPS)


def _head_sums(x):
    wdt = x.shape[1]
    shift = HEAD_DIM.bit_length() - 1
    same = (lax.shift_right_logical(lax.broadcasted_iota(jnp.int32, (wdt, wdt), 0), shift)
            == lax.shift_right_logical(lax.broadcasted_iota(jnp.int32, (wdt, wdt), 1), shift))
    ones = jnp.where(same, 1.0, 0.0).astype(BF16)
    hi = x.astype(BF16)
    lo = (x - hi.astype(F32)).astype(BF16)
    return _dot(hi, ones) + _dot(lo, ones)


def _ret_prompt_kernel(x_ref, cos_ref, sin_ref, dmat_ref, qdec_ref, kdec_ref, gc_ref, ng_ref,
                       out_ref, ns_ref, s_scr, *, r, c):
    t = pl.program_id(1)
    last = pl.num_programs(1) - 1

    @pl.when(t == 0)
    def _():
        s_scr[...] = jnp.zeros(s_scr.shape, F32)

    w = RET_WIDTH
    nc = r // c
    cos = cos_ref[...]
    sin = sin_ref[...]
    q = _rope(x_ref[:, 0:w], cos, sin) * (HEAD_DIM ** -0.5)
    k = _rope(x_ref[:, w:2 * w], cos, sin)
    v = x_ref[:, 2 * w:3 * w]
    qd = q * qdec_ref[...]
    kd = k * kdec_ref[...]
    rows = lambda n: slice(n * c, (n + 1) * c)

    def stack(a):
        return jnp.stack([a[rows(n), _head_slice(h)] for n in range(nc) for h in range(N_HEADS)], axis=0)

    qs, ks, vs = stack(q).astype(BF16), stack(k).astype(BF16), stack(v).astype(BF16)
    scores = _bmm('bid,bjd->bij', qs, ks) * dmat_ref[...]
    o_inner = _bmm('bij,bje->bie', scores.astype(BF16), vs)
    kv = _bmm('bid,bie->bde', stack(kd).astype(BF16), vs)
    gc = jnp.stack([gc_ref[:, _head_slice(h)] for h in range(N_HEADS)], axis=0)
    s = s_scr[...]
    starts = []
    for n in range(nc):
        starts.append(s)
        s = gc * s + kv[n * N_HEADS:(n + 1) * N_HEADS]
    s_scr[...] = s

    @pl.when(t == last)
    def _():
        ns_ref[0] = s

    o = o_inner + _bmm('bid,bde->bie', stack(qd).astype(BF16), jnp.concatenate(starts, axis=0).astype(BF16))
    on = _head_rms(o)
    g = x_ref[:, 3 * w:4 * w]
    for n in range(nc):
        for h in range(N_HEADS):
            hs = _head_slice(h)
            out_ref[rows(n), hs] = on[n * N_HEADS + h] * ng_ref[:, hs] * _silu(g[rows(n), hs])


def _ret_sample_kernel(x_ref, cos_ref, sin_ref, dmat_ref, qdec_ref, kdec_ref, gc_ref, ng_ref, s0_ref,
                       out_ref, ns_ref, *, nseq, c):
    w = RET_WIDTH
    cos = cos_ref[...]
    sin = sin_ref[...]
    q = _rope(x_ref[:, 0:w], cos, sin) * (HEAD_DIM ** -0.5)
    k = _rope(x_ref[:, w:2 * w], cos, sin)
    v = x_ref[:, 2 * w:3 * w]
    qd = q * qdec_ref[...]
    kd = k * kdec_ref[...]
    g = x_ref[:, 3 * w:4 * w]
    per_seq = lambda a: a.reshape(nseq, c, HEAD_DIM)
    heads = range(N_HEADS)
    vb = [v[:, _head_slice(h)].astype(BF16) for h in heads]
    scores = [_dot_nt(q[:, _head_slice(h)].astype(BF16), k[:, _head_slice(h)].astype(BF16)) * dmat_ref[h]
              for h in heads]
    o_inner = [_dot(scores[h].astype(BF16), vb[h]) for h in heads]
    s0 = [s0_ref[:, h] for h in heads]
    o_cross = [_bmm('bqd,bde->bqe', per_seq(qd[:, _head_slice(h)]), s0[h]) for h in heads]
    kv = [_bmm('bkd,bke->bde', per_seq(kd[:, _head_slice(h)]), per_seq(v[:, _head_slice(h)])) for h in heads]
    for h in heads:
        hs = _head_slice(h)
        ns_ref[:, h] = gc_ref[:, hs] * s0[h] + kv[h]
        o = o_inner[h] + o_cross[h].reshape(nseq * c, HEAD_DIM)
        out_ref[:, hs] = _head_rms(o) * ng_ref[:, hs] * _silu(g[:, hs])


def _rope_tables(pos0, seq):
    inv = np.exp(-math.log(ROPE_BASE) * np.arange(HALF, dtype=np.float64) / HALF)
    ang = (pos0 + np.arange(seq, dtype=np.float64))[:, None] * inv[None]
    cos = np.tile(np.concatenate([np.cos(ang), np.cos(ang)], axis=1), (1, N_HEADS))
    sin = np.tile(np.concatenate([-np.sin(ang), np.sin(ang)], axis=1), (1, N_HEADS))
    return jnp.asarray(cos, dtype=F32), jnp.asarray(sin, dtype=F32)


def _ret_decay_tables(c):
    hh = np.arange(N_HEADS, dtype=np.float64)
    lg = np.log(1.0 - 2.0 ** (-5.0 - hh))
    n = np.arange(c, dtype=np.float64)
    rel = n[:, None] - n[None, :]
    dmat = np.where(rel[None] >= 0, np.exp(np.maximum(rel, 0.0)[None] * lg[:, None, None]), 0.0)
    rep = lambda a: np.repeat(a, HEAD_DIM, axis=-1)
    qdec = rep(np.exp((n + 1.0)[:, None] * lg[None, :]))
    kdec = rep(np.exp((c - 1.0 - n)[:, None] * lg[None, :]))
    gc = rep(np.exp(c * lg)[None, :])
    return dmat, qdec, kdec, gc


def _ret(x, ng, state, *, batch, seq, pos0):
    w = RET_WIDTH
    n = batch * seq
    c = math.gcd(seq, CHUNK)
    dmat, qdec, kdec, gc = _ret_decay_tables(c)
    cos, sin = _rope_tables(pos0, seq)
    f = lambda a: jnp.asarray(a, dtype=F32)
    if state is None:
        r = PROMPT_BLOCK
        nc = r // c
        nt = seq // r
        rows = lambda b, t: (b * nt + t, 0)
        trow = lambda b, t: (t, 0)
        const = lambda b, t: (0, 0)
        const3 = lambda b, t: (0, 0, 0)
        st = lambda b, t: (b, 0, 0, 0)
        out, ns = pl.pallas_call(
            functools.partial(_ret_prompt_kernel, r=r, c=c),
            grid=(batch, nt),
            in_specs=[pl.BlockSpec((r, RET_COLS), rows),
                      pl.BlockSpec((r, w), trow),
                      pl.BlockSpec((r, w), trow),
                      pl.BlockSpec((nc * N_HEADS, c, c), const3),
                      pl.BlockSpec((r, w), const),
                      pl.BlockSpec((r, w), const),
                      pl.BlockSpec((1, w), const),
                      pl.BlockSpec((1, w), const)],
            out_specs=[pl.BlockSpec((r, w), rows),
                       pl.BlockSpec((1, N_HEADS, HEAD_DIM, HEAD_DIM), st)],
            out_shape=[jax.ShapeDtypeStruct((n, w), F32),
                       jax.ShapeDtypeStruct((batch, N_HEADS, HEAD_DIM, HEAD_DIM), F32)],
            scratch_shapes=[pltpu.VMEM((N_HEADS, HEAD_DIM, HEAD_DIM), F32)],
            compiler_params=_cparams(("parallel", "arbitrary")),
            name="ret_prompt",
        )(x, cos, sin, f(np.tile(dmat, (nc, 1, 1))), f(np.tile(qdec, (nc, 1))), f(np.tile(kdec, (nc, 1))),
          f(gc), ng)
        return out, ns
    nseq = SAMPLE_SEQS
    r = nseq * c
    seg = np.arange(r) // c
    same = (seg[:, None] == seg[None, :])[None]
    dblk = np.where(same, np.tile(dmat, (1, nseq, nseq)), 0.0)
    rows = lambda i: (i, 0)
    const = lambda i: (0, 0)
    const3 = lambda i: (0, 0, 0)
    st = lambda i: (i, 0, 0, 0)
    sblk = (nseq, N_HEADS, HEAD_DIM, HEAD_DIM)
    out, ns = pl.pallas_call(
        functools.partial(_ret_sample_kernel, nseq=nseq, c=c),
        grid=(batch // nseq,),
        in_specs=[pl.BlockSpec((r, RET_COLS), rows),
                  pl.BlockSpec((r, w), const),
                  pl.BlockSpec((r, w), const),
                  pl.BlockSpec((N_HEADS, r, r), const3),
                  pl.BlockSpec((r, w), const),
                  pl.BlockSpec((r, w), const),
                  pl.BlockSpec((1, w), const),
                  pl.BlockSpec((1, w), const),
                  pl.BlockSpec(sblk, st)],
        out_specs=[pl.BlockSpec((r, w), rows), pl.BlockSpec(sblk, st)],
        out_shape=[jax.ShapeDtypeStruct((n, w), F32),
                   jax.ShapeDtypeStruct((batch, N_HEADS, HEAD_DIM, HEAD_DIM), F32)],
        compiler_params=_cparams(("parallel",)),
        name="ret_sample",
    )(x, jnp.tile(cos, (nseq, 1)), jnp.tile(sin, (nseq, 1)), f(dblk), f(np.tile(qdec, (nseq, 1))),
      f(np.tile(kdec, (nseq, 1))), f(gc), ng, state)
    return out, ns


def _unit_lower_inverse(a, eye, c):
    inv = eye - a
    x = a
    cover = 2
    while cover < c // 2:
        xm = x.astype(BF16)
        x = _bmm('bij,bjk->bik', xm, xm)
        inv = inv + _bmm('bij,bjk->bik', inv.astype(BF16), x.astype(BF16))
        cover *= 2
    resid = eye - inv - _bmm3('bij,bjk->bik', a, inv)
    return inv + _bmm('bij,bjk->bik', inv.astype(BF16), resid.astype(BF16))


def _gdn_gates(ab, alog_ref, dtb_ref, tri_l, tri_u):
    g_log = -jnp.exp(alog_ref[...]) * _softplus(ab + dtb_ref[...])
    beta = jax.nn.sigmoid(ab)
    hi = lax.Precision.HIGHEST
    g_col = _dot(tri_l.astype(F32), g_log, precision=hi)
    g_row = _dot_tn(g_log, tri_u.astype(F32), precision=hi)
    return g_col, g_row, beta


def _l2norm_heads(x, base):
    xs = x[:, base:base + GDN_WIDTH]
    xn = xs * lax.rsqrt(_head_sums(xs * xs) + EPS)
    return [xn[:, _head_slice(h)] for h in range(N_HEADS)]


def _gdn_prompt_kernel(x_ref, cw_ref, alog_ref, dtb_ref, ng_ref, out_ref, nconv_ref, ns_ref, ext, s_scr,
                       *, r, c):
    t = pl.program_id(1)
    last = pl.num_programs(1) - 1
    w = GDN_WIDTH
    hist = SUBLANES - (CONV_W - 1)
    nc = r // c

    @pl.when(t == 0)
    def _():
        ext[0:SUBLANES, :] = jnp.zeros((SUBLANES, 3 * w), F32)
        s_scr[...] = jnp.zeros(s_scr.shape, F32)

    qkv = _silu(_conv_unit(ext, x_ref[:, 0:3 * w], cw_ref, r))

    @pl.when(t == last)
    def _():
        nconv_ref[0] = ext[r + hist:r + SUBLANES, :]

    ext[0:SUBLANES, :] = ext[r:r + SUBLANES, :]
    tri_l, _, tri_u = _segment_masks(r, c)
    g_col, g_row, beta = _gdn_gates(x_ref[:, 4 * w:4 * w + LANES], alog_ref, dtb_ref, tri_l, tri_u)
    qn = [a * (HEAD_DIM ** -0.5) for a in _l2norm_heads(qkv, 0)]
    kn = _l2norm_heads(qkv, w)
    rows = lambda n: slice(n * c, (n + 1) * c)

    def stack(fn):
        return jnp.stack([fn(n, h) for n in range(nc) for h in range(N_HEADS)], axis=0)

    q = stack(lambda n, h: qn[h][rows(n)])
    k = stack(lambda n, h: kn[h][rows(n)])
    v = stack(lambda n, h: qkv[rows(n), _head_slice(h, 2 * w)])
    gi = stack(lambda n, h: g_col[rows(n), h:h + 1])
    gj = stack(lambda n, h: g_row[h:h + 1, rows(n)])
    bi = stack(lambda n, h: beta[rows(n), GDN_BETA_LANE + h:GDN_BETA_LANE + h + 1])
    gl = stack(lambda n, h: g_col[(n + 1) * c - 1:(n + 1) * c, h:h + 1])
    lower, strict, _ = _segment_masks(c, c)
    eye = (lower & jnp.logical_not(strict)).astype(F32)[None]
    decay = jnp.exp(jnp.where(lower[None], gi - gj, -jnp.inf))
    kb = k.astype(BF16)
    a_mat = jnp.where(strict[None], bi * _bmm('bid,bjd->bij', kb, kb) * decay, 0.0)
    tm = _unit_lower_inverse(a_mat, eye, c)
    eg = jnp.exp(gi)
    rhs = jnp.concatenate([bi * v, (bi * eg) * k], axis=2)
    uw = _bmm('bij,bjk->bik', tm.astype(BF16), rhs.astype(BF16))
    uu = uw[:, :, 0:HEAD_DIM]
    pm = (_bmm('bid,bjd->bij', q.astype(BF16), kb) * decay).astype(BF16)
    wq = jnp.concatenate([uw[:, :, HEAD_DIM:2 * HEAD_DIM], q * eg], axis=1).astype(BF16)
    kd = (k * jnp.exp(gl - gi)).astype(BF16)
    egl = jnp.exp(gl)
    z = x_ref[:, 3 * w:4 * w]
    s = s_scr[...]
    for n in range(nc):
        ps = slice(n * N_HEADS, (n + 1) * N_HEADS)
        wqs = _bmm('hid,hde->hie', wq[ps], s.astype(BF16))
        vn = (uu[ps] - wqs[:, 0:c]).astype(BF16)
        o = wqs[:, c:2 * c] + _bmm('hij,hje->hie', pm[ps], vn)
        s = egl[ps] * s + _bmm('hid,hie->hde', kd[ps], vn)
        on = _head_rms(o)
        for h in range(N_HEADS):
            hs = _head_slice(h)
            out_ref[rows(n), hs] = on[h] * ng_ref[:, hs] * _silu(z[rows(n), hs])
    s_scr[...] = s

    @pl.when(t == last)
    def _():
        ns_ref[0] = s


def _gdn_sample_kernel(x_ref, cw_ref, alog_ref, dtb_ref, ng_ref, buf_ref, s0_ref,
                       out_ref, nconv_ref, ns_ref, ext, y_scr, *, nseq, c):
    w = GDN_WIDTH
    hist = SUBLANES - (CONV_W - 1)
    r = nseq * c

    def conv_body(u, carry_):
        r0 = pl.multiple_of(u * c, SUBLANES)
        ext[hist:SUBLANES, :] = buf_ref[u]
        y_scr[pl.ds(r0, c), :] = _conv_unit(ext, x_ref[pl.ds(r0, c), 0:3 * w], cw_ref, c)
        nconv_ref[u] = ext[c + hist:c + SUBLANES, :]
        return carry_

    lax.fori_loop(0, nseq, conv_body, 0)
    qkv = _silu(y_scr[...])
    lower, strict, upper = _segment_masks(r, c)
    g_col, g_row, beta = _gdn_gates(x_ref[:, 4 * w:4 * w + LANES], alog_ref, dtb_ref, lower, upper)
    g_seq = g_col.reshape(nseq, c, LANES)
    g_end = g_seq[:, c - 1:c, :]
    g_end_rows = jnp.broadcast_to(g_end, (nseq, c, LANES)).reshape(r, LANES)
    qn = [a * (HEAD_DIM ** -0.5) for a in _l2norm_heads(qkv, 0)]
    kn = _l2norm_heads(qkv, w)
    heads = range(N_HEADS)
    hstack = lambda fn: jnp.stack([fn(h) for h in heads], axis=0)
    q = hstack(lambda h: qn[h])
    k = hstack(lambda h: kn[h])
    v = hstack(lambda h: qkv[:, _head_slice(h, 2 * w)])
    gi = hstack(lambda h: g_col[:, h:h + 1])
    gj = hstack(lambda h: g_row[h:h + 1, :])
    bi = hstack(lambda h: beta[:, GDN_BETA_LANE + h:GDN_BETA_LANE + h + 1])
    ge = hstack(lambda h: g_end_rows[:, h:h + 1])
    eye = (lower & jnp.logical_not(strict)).astype(F32)[None]
    decay = jnp.exp(jnp.where(lower[None], gi - gj, -jnp.inf))
    kb = k.astype(BF16)
    a_mat = jnp.where(strict[None], bi * _bmm('bid,bjd->bij', kb, kb) * decay, 0.0)
    tm = _unit_lower_inverse(a_mat, eye, c)
    eg = jnp.exp(gi)
    rhs = jnp.concatenate([bi * v, (bi * eg) * k], axis=2)
    uw = _bmm('bij,bjk->bik', tm.astype(BF16), rhs.astype(BF16))
    pm = (_bmm('bid,bjd->bij', q.astype(BF16), kb) * decay).astype(BF16)
    qg = q * eg
    kd = k * jnp.exp(ge - gi)
    per_seq = lambda a: a.reshape(nseq, c, HEAD_DIM)
    s0 = [s0_ref[:, h] for h in heads]
    wqs = [_bmm('bid,bde->bie',
                jnp.concatenate([per_seq(uw[h][:, HEAD_DIM:2 * HEAD_DIM]), per_seq(qg[h])], axis=1), s0[h])
           for h in heads]
    vn = [per_seq(uw[h][:, 0:HEAD_DIM]) - wqs[h][:, 0:c] for h in heads]
    o = [wqs[h][:, c:2 * c].reshape(r, HEAD_DIM)
         + _dot(pm[h], vn[h].reshape(r, HEAD_DIM).astype(BF16)) for h in heads]
    upd = [_bmm('bid,bie->bde', per_seq(kd[h]), vn[h]) for h in heads]
    z = x_ref[:, 3 * w:4 * w]
    for h in heads:
        hs = _head_slice(h)
        ns_ref[:, h] = jnp.exp(g_end[:, :, h:h + 1]) * s0[h] + upd[h]
        out_ref[:, hs] = _head_rms(o[h]) * ng_ref[:, hs] * _silu(z[:, hs])


def _gdn(x, cw, alog, dtb, ng, state, *, batch, seq):
    w = GDN_WIDTH
    n = batch * seq
    c = math.gcd(seq, CHUNK)
    if state is None:
        r = PROMPT_BLOCK
        nt = seq // r
        rows = lambda b, t: (b * nt + t, 0)
        const = lambda b, t: (0, 0)
        st3 = lambda b, t: (b, 0, 0)
        st4 = lambda b, t: (b, 0, 0, 0)
        return pl.pallas_call(
            functools.partial(_gdn_prompt_kernel, r=r, c=c),
            grid=(batch, nt),
            in_specs=[pl.BlockSpec((r, GDN_COLS), rows),
                      pl.BlockSpec((CONV_W, 3 * w), const),
                      pl.BlockSpec((1, LANES), const),
                      pl.BlockSpec((1, LANES), const),
                      pl.BlockSpec((1, w), const)],
            out_specs=[pl.BlockSpec((r, w), rows),
                       pl.BlockSpec((1, CONV_W - 1, 3 * w), st3),
                       pl.BlockSpec((1, N_HEADS, HEAD_DIM, HEAD_DIM), st4)],
            out_shape=[jax.ShapeDtypeStruct((n, w), F32),
                       jax.ShapeDtypeStruct((batch, CONV_W - 1, 3 * w), F32),
                       jax.ShapeDtypeStruct((batch, N_HEADS, HEAD_DIM, HEAD_DIM), F32)],
            scratch_shapes=[pltpu.VMEM((r + SUBLANES, 3 * w), F32),
                            pltpu.VMEM((N_HEADS, HEAD_DIM, HEAD_DIM), F32)],
            compiler_params=_cparams(("parallel", "arbitrary")),
            name="gdn_prompt",
        )(x, cw, alog, dtb, ng)
    nseq = SAMPLE_SEQS
    r = nseq * c
    rows = lambda i: (i, 0)
    const = lambda i: (0, 0)
    st3 = lambda i: (i, 0, 0)
    st4 = lambda i: (i, 0, 0, 0)
    cblk = (nseq, CONV_W - 1, 3 * w)
    sblk = (nseq, N_HEADS, HEAD_DIM, HEAD_DIM)
    return pl.pallas_call(
        functools.partial(_gdn_sample_kernel, nseq=nseq, c=c),
        grid=(batch // nseq,),
        in_specs=[pl.BlockSpec((r, GDN_COLS), rows),
                  pl.BlockSpec((CONV_W, 3 * w), const),
                  pl.BlockSpec((1, LANES), const),
                  pl.BlockSpec((1, LANES), const),
                  pl.BlockSpec((1, w), const),
                  pl.BlockSpec(cblk, st3),
                  pl.BlockSpec(sblk, st4)],
        out_specs=[pl.BlockSpec((r, w), rows), pl.BlockSpec(cblk, st3), pl.BlockSpec(sblk, st4)],
        out_shape=[jax.ShapeDtypeStruct((n, w), F32),
                   jax.ShapeDtypeStruct((batch, CONV_W - 1, 3 * w), F32),
                   jax.ShapeDtypeStruct((batch, N_HEADS, HEAD_DIM, HEAD_DIM), F32)],
        scratch_shapes=[pltpu.VMEM((c + SUBLANES, 3 * w), F32), pltpu.VMEM((r, 3 * w), F32)],
        compiler_params=_cparams(("parallel",)),
        name="gdn_sample",
    )(x, cw, alog, dtb, ng, *state)


def _block_diag(wb):
    out = jnp.zeros((LRU_WIDTH, LRU_WIDTH), wb.dtype)
    for n in range(LRU_BLOCKS):
        s = slice(n * LRU_BLOCK, (n + 1) * LRU_BLOCK)
        out = out.at[s, s].set(wb[n])
    return out


def _pad_lanes(v, offset=0):
    return jnp.zeros((1, LANES), F32).at[0, offset:offset + v.shape[0]].set(v.astype(F32))


def _group_cols(w):
    w = w.reshape(N_GROUPS, EXPERTS_PER_GROUP, D_MODEL, D_EXPERT)
    return jnp.transpose(w, (0, 2, 1, 3)).reshape(N_GROUPS, D_MODEL, D_GROUP).astype(BF16)


def _prep_layer(W, l):
    row = lambda v: v.reshape(1, -1).astype(F32)
    pad = IN_COLS - W['w_in'].shape[2]
    wr = jnp.zeros((D_MODEL, LANES), F32)
    wr = wr.at[:, ROUTER_GROUP_LANE:ROUTER_GROUP_LANE + N_GROUPS].set(W['w_router_group'][l])
    wr = wr.at[:, ROUTER_EXPERT_LANE:ROUTER_EXPERT_LANE + N_EXPERTS].set(W['w_router_expert'][l])
    br = jnp.zeros((1, LANES), F32)
    br = br.at[0, ROUTER_GROUP_LANE:ROUTER_GROUP_LANE + N_GROUPS].set(W['b_router_group'][l])
    br = br.at[0, ROUTER_EXPERT_LANE:ROUTER_EXPERT_LANE + N_EXPERTS].set(W['b_router_expert'][l])
    return dict(
        norm_mix_g=row(W['norm_mix_g'][l]),
        w_in=jnp.pad(W['w_in'][l], ((0, 0), (0, pad))).astype(BF16),
        lru_conv_w=W['lru_conv_w'][l], lru_conv_b=row(W['lru_conv_b'][l]),
        lru_wa=_block_diag(W['lru_wa'][l]).astype(BF16), lru_ba=row(W['lru_ba'][l]),
        lru_wx=_block_diag(W['lru_wx'][l]).astype(BF16), lru_bx=row(W['lru_bx'][l]),
        lru_lambda=row(W['lru_lambda'][l]),
        ret_norm_g=row(W['ret_norm_g'][l]),
        gdn_conv_w=W['gdn_conv_w'][l],
        gdn_a_log=_pad_lanes(W['gdn_a_log'][l]), gdn_dt_bias=_pad_lanes(W['gdn_dt_bias'][l]),
        gdn_norm_g=jnp.tile(row(W['gdn_norm_g'][l]), (1, N_HEADS)),
        w_out=W['w_out'][l].astype(BF16),
        norm_ffn_g=row(W['norm_ffn_g'][l]),
        w_router=wr.astype(BF16), b_router=br,
        w_expert_gate=_group_cols(W['w_expert_gate'][l]),
        w_expert_up=_group_cols(W['w_expert_up'][l]),
        w_expert_down=W['w_expert_down'][l].reshape(N_GROUPS, D_GROUP, D_MODEL).astype(BF16),
        norm_pe_g=row(W['norm_pe_g'][l]),
        w_pe=W['w_pe'][l].astype(BF16),
        w_pe_gate=W['w_pe_gate'][l].astype(BF16),
    )


def _trunk(x, p, states, pos0, layers, final_g):
    batch, seq, _ = x.shape
    n = batch * seq
    h = x.reshape(n, D_MODEL)
    new = ([], [], [], [], [])
    for l, L in enumerate(layers):
        if states is None:
            st_lru = st_ret = st_gdn = None
        else:
            st_lru = (states[0][l], states[1][l])
            st_ret = states[2][l]
            st_gdn = (states[3][l], states[4][l])
        lru_x, ret_x, gdn_x = _in_proj(h, L['norm_mix_g'], L['w_in'])
        out_a, n_lru_conv, n_lru_h = _lru(
            lru_x, L['lru_conv_w'], L['lru_conv_b'], L['lru_wa'], L['lru_ba'], L['lru_wx'],
            L['lru_bx'], L['lru_lambda'], st_lru, batch=batch, seq=seq)
        out_b, n_ret = _ret(ret_x, L['ret_norm_g'], st_ret, batch=batch, seq=seq, pos0=pos0)
        out_c, n_gdn_conv, n_gdn = _gdn(
            gdn_x, L['gdn_conv_w'], L['gdn_a_log'], L['gdn_dt_bias'], L['gdn_norm_g'], st_gdn,
            batch=batch, seq=seq)
        h1, xn, gates = _out_router(h, out_a, out_b, out_c, L['w_out'], L['norm_ffn_g'],
                                    L['w_router'], L['b_router'])
        h2 = _moe(xn, gates, h1, L['w_expert_gate'], L['w_expert_up'], L['w_expert_down'])
        h = _pe(h2, p[l].reshape(n, PE_DIM), L['norm_pe_g'], L['w_pe_gate'], L['w_pe'], final_g,
                final=(l == len(layers) - 1))
        for lst, s in zip(new, (n_lru_conv, n_lru_h, n_ret, n_gdn_conv, n_gdn)):
            lst.append(s)
    return h.reshape(batch, seq, D_MODEL), tuple(jnp.stack(lst) for lst in new)


def kernel(x_prompt, x_sample, p_prompt, p_sample, state_lru_conv, state_lru_h, state_ret, state_gdn_conv, state_gdn, norm_mix_g, w_in, lru_conv_w, lru_conv_b, lru_wa, lru_ba, lru_wx, lru_bx, lru_lambda, ret_norm_g, gdn_conv_w, gdn_a_log, gdn_dt_bias, gdn_norm_g, w_out, norm_ffn_g, w_router_group, b_router_group, w_router_expert, b_router_expert, w_expert_gate, w_expert_up, w_expert_down, norm_pe_g, w_pe, w_pe_gate, final_norm_g):
    W = dict(norm_mix_g=norm_mix_g, w_in=w_in, lru_conv_w=lru_conv_w, lru_conv_b=lru_conv_b,
             lru_wa=lru_wa, lru_ba=lru_ba, lru_wx=lru_wx, lru_bx=lru_bx, lru_lambda=lru_lambda,
             ret_norm_g=ret_norm_g, gdn_conv_w=gdn_conv_w, gdn_a_log=gdn_a_log, gdn_dt_bias=gdn_dt_bias,
             gdn_norm_g=gdn_norm_g, w_out=w_out, norm_ffn_g=norm_ffn_g, w_router_group=w_router_group,
             b_router_group=b_router_group, w_router_expert=w_router_expert, b_router_expert=b_router_expert,
             w_expert_gate=w_expert_gate, w_expert_up=w_expert_up, w_expert_down=w_expert_down,
             norm_pe_g=norm_pe_g, w_pe=w_pe, w_pe_gate=w_pe_gate)
    layers = [_prep_layer(W, l) for l in range(DEPTH)]
    final_g = final_norm_g.reshape(1, D_MODEL).astype(F32)
    y_p, st_p = _trunk(x_prompt, p_prompt, None, 0, layers, final_g)
    sample_states = (state_lru_conv, state_lru_h, state_ret, state_gdn_conv, state_gdn)
    y_s, st_s = _trunk(x_sample, p_sample, sample_states, PAST_LEN, layers, final_g)
    return (y_p, y_s) + st_p + st_s
```

```python
import functools
import math

import jax
import jax.numpy as jnp
import numpy as np
from jax import lax
from jax.experimental import pallas as pl
from jax.experimental.pallas import tpu as pltpu

F32 = jnp.float32
BF16 = jnp.bfloat16

D_MODEL = 1024
DEPTH = 2
PAST_LEN = 16384
HEAD_DIM = 64
HALF = HEAD_DIM // 2
LRU_WIDTH = 256
LRU_BLOCKS = 4
LRU_BLOCK = 64
LRU_C = 8.0
RET_WIDTH = 384
GDN_WIDTH = 384
N_HEADS = 6
CONV_W = 4
CHUNK = 64
ROPE_BASE = 10000.0
PE_DIM = 256
N_GROUPS = 4
EXPERTS_PER_GROUP = 4
N_EXPERTS = 16
D_EXPERT = 256
EPS = 1e-6

LANES = 128
SUBLANES = 8
VMEM_LIMIT = 48 * 1024 * 1024

LRU_COLS = 2 * LRU_WIDTH
RET_COLS = 4 * RET_WIDTH
GDN_COLS = 4 * GDN_WIDTH + LANES
IN_COLS = LRU_COLS + RET_COLS + GDN_COLS
ROUTER_GROUP_LANE = 0
ROUTER_EXPERT_LANE = N_GROUPS
GDN_BETA_LANE = N_HEADS

D_GROUP = EXPERTS_PER_GROUP * D_EXPERT

TOKEN_TILE = 512
MOE_TILE = 1024
PROMPT_BLOCK = 256
SAMPLE_SEQS = 16


def _cparams(sem):
    return pltpu.CompilerParams(dimension_semantics=sem, vmem_limit_bytes=VMEM_LIMIT)


def _rmsnorm(x, g):
    return x * lax.rsqrt(jnp.mean(x * x, axis=-1, keepdims=True) + EPS) * g


def _dot(a, b, **kw):
    return jnp.dot(a, b, preferred_element_type=F32, **kw)


def _dot_nt(a, b):
    return lax.dot_general(a, b, (((1,), (1,)), ((), ())), preferred_element_type=F32)


def _dot_tn(a, b, **kw):
    return lax.dot_general(a, b, (((0,), (0,)), ((), ())), preferred_element_type=F32, **kw)


def _bmm(spec, a, b):
    return jnp.einsum(spec, a, b, preferred_element_type=F32)


def _bmm3(spec, a, b):
    a_hi = a.astype(BF16)
    a_lo = (a - a_hi.astype(F32)).astype(BF16)
    b_hi = b.astype(BF16)
    b_lo = (b - b_hi.astype(F32)).astype(BF16)
    return _bmm(spec, a_hi, b_hi) + (_bmm(spec, a_hi, b_lo) + _bmm(spec, a_lo, b_hi))


def _softplus(x):
    return jnp.maximum(x, 0.0) + jnp.log1p(jnp.exp(-jnp.abs(x)))


def _silu(x):
    return x * jax.nn.sigmoid(x)


def _gelu_tanh(x):
    return 0.5 * x * (1.0 + jnp.tanh(math.sqrt(2.0 / math.pi) * (x + 0.044715 * (x * x * x))))


def _in_proj_kernel(h_ref, g_ref, w_ref, lru_ref, ret_ref, gdn_ref):
    xn = _rmsnorm(h_ref[...], g_ref[...]).astype(BF16)
    lru_ref[...] = _dot(xn, w_ref[:, 0:LRU_COLS])
    ret_ref[...] = _dot(xn, w_ref[:, LRU_COLS:LRU_COLS + RET_COLS])
    gdn_ref[...] = _dot(xn, w_ref[:, LRU_COLS + RET_COLS:IN_COLS])


def _in_proj(h, g, w):
    n = h.shape[0]
    tm = TOKEN_TILE
    row = lambda i: (i, 0)
    const = lambda i: (0, 0)
    return pl.pallas_call(
        _in_proj_kernel,
        grid=(n // tm,),
        in_specs=[pl.BlockSpec((tm, D_MODEL), row),
                  pl.BlockSpec((1, D_MODEL), const),
                  pl.BlockSpec((D_MODEL, IN_COLS), const)],
        out_specs=[pl.BlockSpec((tm, LRU_COLS), row),
                   pl.BlockSpec((tm, RET_COLS), row),
                   pl.BlockSpec((tm, GDN_COLS), row)],
        out_shape=[jax.ShapeDtypeStruct((n, LRU_COLS), F32),
                   jax.ShapeDtypeStruct((n, RET_COLS), F32),
                   jax.ShapeDtypeStruct((n, GDN_COLS), F32)],
        compiler_params=_cparams(("parallel",)),
        name="in_proj",
    )(h, g, w)


def _route(logits):
    lane = lax.broadcasted_iota(jnp.int32, logits.shape, 1).astype(F32)
    neg = -jnp.inf
    far = float(LANES)
    gmask = lane < N_GROUPS
    gl = jnp.where(gmask, logits, neg)
    gmax = jnp.max(gl, axis=-1, keepdims=True)
    gsel = jnp.min(jnp.where(gl == gmax, lane, far), axis=-1, keepdims=True)
    gsum = jnp.sum(jnp.where(gmask, jnp.exp(logits - gmax), 0.0), axis=-1, keepdims=True)
    g_w = 1.0 / gsum
    lo = ROUTER_EXPERT_LANE + EXPERTS_PER_GROUP * gsel
    el = jnp.where(lane >= lo, jnp.where(lane < lo + EXPERTS_PER_GROUP, logits, neg), neg)
    m1 = jnp.max(el, axis=-1, keepdims=True)
    i1 = jnp.min(jnp.where(el == m1, lane, far), axis=-1, keepdims=True)
    el2 = jnp.where(lane == i1, neg, el)
    m2 = jnp.max(el2, axis=-1, keepdims=True)
    i2 = jnp.min(jnp.where(el2 == m2, lane, far), axis=-1, keepdims=True)
    e2 = jnp.exp(m2 - m1)
    w1 = g_w / (1.0 + e2)
    w2 = w1 * e2
    return jnp.where(lane == i1, w1, 0.0) + jnp.where(lane == i2, w2, 0.0)


def _out_router_kernel(h_ref, a_ref, b_ref, c_ref, wo_ref, g_ref, wr_ref, br_ref,
                       h1_ref, xn_ref, gates_ref):
    o1 = LRU_WIDTH
    o2 = LRU_WIDTH + RET_WIDTH
    mix = (_dot(a_ref[...].astype(BF16), wo_ref[0:o1, :])
           + _dot(b_ref[...].astype(BF16), wo_ref[o1:o2, :])
           + _dot(c_ref[...].astype(BF16), wo_ref[o2:D_MODEL, :]))
    h1 = h_ref[...] + mix
    h1_ref[...] = h1
    xn = _rmsnorm(h1, g_ref[...]).astype(BF16)
    xn_ref[...] = xn
    gates_ref[...] = _route(_dot(xn, wr_ref[...]) + br_ref[...])


def _out_router(h, a, b, c, wo, g, wr, br):
    n = h.shape[0]
    tm = TOKEN_TILE
    row = lambda i: (i, 0)
    const = lambda i: (0, 0)
    return pl.pallas_call(
        _out_router_kernel,
        grid=(n // tm,),
        in_specs=[pl.BlockSpec((tm, D_MODEL), row),
                  pl.BlockSpec((tm, LRU_WIDTH), row),
                  pl.BlockSpec((tm, RET_WIDTH), row),
                  pl.BlockSpec((tm, GDN_WIDTH), row),
                  pl.BlockSpec((D_MODEL, D_MODEL), const),
                  pl.BlockSpec((1, D_MODEL), const),
                  pl.BlockSpec((D_MODEL, LANES), const),
                  pl.BlockSpec((1, LANES), const)],
        out_specs=[pl.BlockSpec((tm, D_MODEL), row),
                   pl.BlockSpec((tm, D_MODEL), row),
                   pl.BlockSpec((tm, LANES), row)],
        out_shape=[jax.ShapeDtypeStruct((n, D_MODEL), F32),
                   jax.ShapeDtypeStruct((n, D_MODEL), BF16),
                   jax.ShapeDtypeStruct((n, LANES), F32)],
        compiler_params=_cparams(("parallel",)),
        name="out_router",
    )(h, a, b, c, wo, g, wr, br)


def _moe_kernel(xn_ref, gates_ref, h1_ref, wg_ref, wu_ref, wd_ref, out_ref):
    grp = pl.program_id(1)

    @pl.when(grp == 0)
    def _():
        out_ref[...] = h1_ref[...]

    xn = xn_ref[...]
    gates = gates_ref[...]
    lane = lax.broadcasted_iota(jnp.int32, gates.shape, 1)
    base = ROUTER_EXPERT_LANE + EXPERTS_PER_GROUP * grp
    parts = []
    for j in range(EXPERTS_PER_GROUP):
        hg = _dot(xn, wg_ref[j])
        hu = _dot(xn, wu_ref[j])
        gcol = jnp.sum(jnp.where(lane == base + j, gates, 0.0), axis=-1, keepdims=True)
        parts.append((_silu(hg) * hu * gcol).astype(BF16))
    out_ref[...] += _dot(jnp.concatenate(parts, axis=1), wd_ref[...].reshape(D_GROUP, D_MODEL))


def _moe(xn, gates, h1, wg, wu, wd):
    n = xn.shape[0]
    tm = MOE_TILE if n % MOE_TILE == 0 else TOKEN_TILE
    row = lambda i, e: (i, 0)
    return pl.pallas_call(
        _moe_kernel,
        grid=(n // tm, N_GROUPS),
        in_specs=[pl.BlockSpec((tm, D_MODEL), row),
                  pl.BlockSpec((tm, LANES), row),
                  pl.BlockSpec((tm, D_MODEL), row),
                  pl.BlockSpec((EXPERTS_PER_GROUP, D_MODEL, D_EXPERT), lambda i, e: (e, 0, 0)),
                  pl.BlockSpec((EXPERTS_PER_GROUP, D_MODEL, D_EXPERT), lambda i, e: (e, 0, 0)),
                  pl.BlockSpec((EXPERTS_PER_GROUP, D_EXPERT, D_MODEL), lambda i, e: (e, 0, 0))],
        out_specs=pl.BlockSpec((tm, D_MODEL), row),
        out_shape=jax.ShapeDtypeStruct((n, D_MODEL), F32),
        compiler_params=_cparams(("parallel", "arbitrary")),
        name="moe",
    )(xn, gates, h1, wg, wu, wd)


def _pe_kernel(h_ref, p_ref, g_ref, wgate_ref, wpe_ref, fg_ref, out_ref, *, final):
    h = h_ref[...]
    xn = _rmsnorm(h, g_ref[...]).astype(BF16)
    gate = jax.nn.sigmoid(_dot(xn, wgate_ref[...]))
    pe = _dot(p_ref[...].astype(BF16), wpe_ref[...])
    h3 = h + pe * gate
    out_ref[...] = _rmsnorm(h3, fg_ref[...]) if final else h3


def _pe(h, p, g, wgate, wpe, fg, final):
    n = h.shape[0]
    tm = TOKEN_TILE
    row = lambda i: (i, 0)
    const = lambda i: (0, 0)
    return pl.pallas_call(
        functools.partial(_pe_kernel, final=final),
        grid=(n // tm,),
        in_specs=[pl.BlockSpec((tm, D_MODEL), row),
                  pl.BlockSpec((tm, PE_DIM), row),
                  pl.BlockSpec((1, D_MODEL), const),
                  pl.BlockSpec((D_MODEL, D_MODEL), const),
                  pl.BlockSpec((PE_DIM, D_MODEL), const),
                  pl.BlockSpec((1, D_MODEL), const)],
        out_specs=pl.BlockSpec((tm, D_MODEL), row),
        out_shape=jax.ShapeDtypeStruct((n, D_MODEL), F32),
        compiler_params=_cparams(("parallel",)),
        name="pe_gate",
    )(h, p, g, wgate, wpe, fg)


def _conv_unit(ext, x, w_ref, c):
    ext[SUBLANES:SUBLANES + c, :] = x
    y = w_ref[CONV_W - 1:CONV_W, :] * x
    for j in range(CONV_W - 1):
        off = SUBLANES - (CONV_W - 1) + j
        y = y + w_ref[j:j + 1, :] * ext[off:off + c, :]
    return y


def _lru_kernel(*refs, c, units, carry):
    if carry:
        (x_ref, cw_ref, cb_ref, wa_ref, ba_ref, wx_ref, bx_ref, lam_ref,
         out_ref, nconv_ref, nh_ref, ext, y_scr, a_scr, b_scr, hcar) = refs
    else:
        (x_ref, cw_ref, cb_ref, wa_ref, ba_ref, wx_ref, bx_ref, lam_ref, buf_ref, h0_ref,
         out_ref, nconv_ref, nh_ref, ext, y_scr, a_scr, b_scr) = refs
    w = LRU_WIDTH
    hist = SUBLANES - (CONV_W - 1)
    if carry:
        t = pl.program_id(1)
        last = pl.num_programs(1) - 1

        @pl.when(t == 0)
        def _():
            ext[0:SUBLANES, :] = jnp.zeros((SUBLANES, w), F32)
            hcar[...] = jnp.zeros((1, w), F32)

    def conv_body(u, carry_):
        r0 = pl.multiple_of(u * c, SUBLANES)
        if not carry:
            ext[hist:SUBLANES, :] = buf_ref[u]
        x = x_ref[pl.ds(r0, c), 0:w]
        y_scr[pl.ds(r0, c), :] = _conv_unit(ext, x, cw_ref, c) + cb_ref[...]
        if carry:
            @pl.when(t == last)
            def _():
                nconv_ref[0] = ext[c + hist:c + SUBLANES, :]
            ext[0:SUBLANES, :] = ext[c:c + SUBLANES, :]
        else:
            nconv_ref[u] = ext[c + hist:c + SUBLANES, :]
        return carry_

    lax.fori_loop(0, units, conv_body, 0)

    y = y_scr[...]
    yb = y.astype(BF16)
    r = jax.nn.sigmoid(_dot(yb, wa_ref[...]) + ba_ref[...])
    i = jax.nn.sigmoid(_dot(yb, wx_ref[...]) + bx_ref[...])
    log_a = (-LRU_C) * r * _softplus(-lam_ref[...])
    th = jnp.tanh(log_a)
    a_scr[...] = jnp.exp(log_a)
    b_scr[...] = jnp.sqrt(-2.0 * th / (1.0 - th)) * (i * y)

    def scan_body(u, carry_):
        r0 = pl.multiple_of(u * c, SUBLANES)
        a = a_scr[pl.ds(r0, c), :]
        b = b_scr[pl.ds(r0, c), :]
        rowi = lax.broadcasted_iota(jnp.int32, (c, w), 0)
        d = 1
        while d < c:
            keep = rowi >= d
            a_s = pltpu.roll(a, d, axis=0)
            b_s = pltpu.roll(b, d, axis=0)
            b = jnp.where(keep, a * b_s + b, b)
            a = jnp.where(keep, a * a_s, a)
            d *= 2
        h0 = hcar[...] if carry else h0_ref[u]
        h = a * h0 + b
        gate = x_ref[pl.ds(r0, c), w:2 * w]
        out_ref[pl.ds(r0, c), :] = h * _gelu_tanh(gate)
        if carry:
            hcar[...] = h[c - 1:c, :]
        else:
            nh_ref[u] = h[c - 1:c, :]
        return carry_

    lax.fori_loop(0, units, scan_body, 0)
    if carry:
        @pl.when(t == last)
        def _():
            nh_ref[0] = hcar[...]


def _lru(x, cw, cb, wa, ba, wx, bx, lam, state, *, batch, seq):
    w = LRU_WIDTH
    carry = state is None
    if carry:
        c, units = PROMPT_BLOCK, 1
        nt = seq // c
        grid = (batch, nt)
        rows = lambda b, t: (b * nt + t, 0)
        const = lambda b, t: (0, 0)
        st = lambda b, t: (b, 0, 0)
        sem = ("parallel", "arbitrary")
        state_in, state_specs = [], []
    else:
        c, units = seq, batch
        grid = (1,)
        rows = lambda i: (0, 0)
        const = lambda i: (0, 0)
        st = lambda i: (0, 0, 0)
        sem = ("arbitrary",)
        state_in = [state[0], state[1].reshape(batch, 1, w)]
        state_specs = [pl.BlockSpec((units, CONV_W - 1, w), st), pl.BlockSpec((units, 1, w), st)]
    n = batch * seq
    blk = c * units
    scratch = [pltpu.VMEM((c + SUBLANES, w), F32), pltpu.VMEM((blk, w), F32),
               pltpu.VMEM((blk, w), F32), pltpu.VMEM((blk, w), F32)]
    if carry:
        scratch.append(pltpu.VMEM((1, w), F32))
    out, nconv, nh = pl.pallas_call(
        functools.partial(_lru_kernel, c=c, units=units, carry=carry),
        grid=grid,
        in_specs=[pl.BlockSpec((blk, LRU_COLS), rows),
                  pl.BlockSpec((CONV_W, w), const),
                  pl.BlockSpec((1, w), const),
                  pl.BlockSpec((w, w), const),
                  pl.BlockSpec((1, w), const),
                  pl.BlockSpec((w, w), const),
                  pl.BlockSpec((1, w), const),
                  pl.BlockSpec((1, w), const)] + state_specs,
        out_specs=[pl.BlockSpec((blk, w), rows),
                   pl.BlockSpec((units, CONV_W - 1, w), st),
                   pl.BlockSpec((units, 1, w), st)],
        out_shape=[jax.ShapeDtypeStruct((n, w), F32),
                   jax.ShapeDtypeStruct((batch, CONV_W - 1, w), F32),
                   jax.ShapeDtypeStruct((batch, 1, w), F32)],
        scratch_shapes=scratch,
        compiler_params=_cparams(sem),
        name="lru_prompt" if carry else "lru_sample",
    )(x, cw, cb, wa, ba, wx, bx, lam, *state_in)
    return out, nconv, nh.reshape(batch, w)


def _rope(x, cos, sin):
    parts = []
    for j in range(x.shape[1] // LANES):
        sl = slice(j * LANES, (j + 1) * LANES)
        xs = x[:, sl]
        lane = lax.broadcasted_iota(jnp.int32, xs.shape, 1)
        swapped = jnp.where((lane & HALF) == 0,
                            pltpu.roll(xs, LANES - HALF, axis=1), pltpu.roll(xs, HALF, axis=1))
        parts.append(xs * cos[:, sl] + swapped * sin[:, sl])
    return jnp.concatenate(parts, axis=1)


def _segment_masks(r, c):
    ri = lax.broadcasted_iota(jnp.int32, (r, r), 0)
    ci = lax.broadcasted_iota(jnp.int32, (r, r), 1)
    if c == r:
        return ri >= ci, ri > ci, ri <= ci
    shift = c.bit_length() - 1
    same = lax.shift_right_logical(ri, shift) == lax.shift_right_logical(ci, shift)
    return same & (ri >= ci), same & (ri > ci), same & (ri <= ci)


def _head_slice(h, base=0):
    return slice(base + h * HEAD_DIM, base + (h + 1) * HEAD_DIM)


def _head_rms(o):
    return o * lax.rsqrt(jnp.mean(o * o, axis=-1, keepdims=True) + EPS)


def _head_sums(x):
    wdt = x.shape[1]
    shift = HEAD_DIM.bit_length() - 1
    same = (lax.shift_right_logical(lax.broadcasted_iota(jnp.int32, (wdt, wdt), 0), shift)
            == lax.shift_right_logical(lax.broadcasted_iota(jnp.int32, (wdt, wdt), 1), shift))
    ones = jnp.where(same, 1.0, 0.0).astype(BF16)
    hi = x.astype(BF16)
    lo = (x - hi.astype(F32)).astype(BF16)
    return _dot(hi, ones) + _dot(lo, ones)


def _ret_prompt_kernel(x_ref, cos_ref, sin_ref, dmat_ref, qdec_ref, kdec_ref, gc_ref, ng_ref,
                       out_ref, ns_ref, s_scr, *, r, c):
    t = pl.program_id(1)
    last = pl.num_programs(1) - 1

    @pl.when(t == 0)
    def _():
        s_scr[...] = jnp.zeros(s_scr.shape, F32)

    w = RET_WIDTH
    nc = r // c
    cos = cos_ref[...]
    sin = sin_ref[...]
    q = _rope(x_ref[:, 0:w], cos, sin) * (HEAD_DIM ** -0.5)
    k = _rope(x_ref[:, w:2 * w], cos, sin)
    v = x_ref[:, 2 * w:3 * w]
    qd = q * qdec_ref[...]
    kd = k * kdec_ref[...]
    rows = lambda n: slice(n * c, (n + 1) * c)

    def stack(a):
        return jnp.stack([a[rows(n), _head_slice(h)] for n in range(nc) for h in range(N_HEADS)], axis=0)

    qs, ks, vs = stack(q).astype(BF16), stack(k).astype(BF16), stack(v).astype(BF16)
    scores = _bmm('bid,bjd->bij', qs, ks) * dmat_ref[...]
    o_inner = _bmm('bij,bje->bie', scores.astype(BF16), vs)
    kv = _bmm('bid,bie->bde', stack(kd).astype(BF16), vs)
    gc = jnp.stack([gc_ref[:, _head_slice(h)] for h in range(N_HEADS)], axis=0)
    s = s_scr[...]
    starts = []
    for n in range(nc):
        starts.append(s)
        s = gc * s + kv[n * N_HEADS:(n + 1) * N_HEADS]
    s_scr[...] = s

    @pl.when(t == last)
    def _():
        ns_ref[0] = s

    o = o_inner + _bmm('bid,bde->bie', stack(qd).astype(BF16), jnp.concatenate(starts, axis=0).astype(BF16))
    on = _head_rms(o)
    g = x_ref[:, 3 * w:4 * w]
    for n in range(nc):
        for h in range(N_HEADS):
            hs = _head_slice(h)
            out_ref[rows(n), hs] = on[n * N_HEADS + h] * ng_ref[:, hs] * _silu(g[rows(n), hs])


def _ret_sample_kernel(x_ref, cos_ref, sin_ref, dmat_ref, qdec_ref, kdec_ref, gc_ref, ng_ref, s0_ref,
                       out_ref, ns_ref, *, nseq, c):
    w = RET_WIDTH
    cos = cos_ref[...]
    sin = sin_ref[...]
    q = _rope(x_ref[:, 0:w], cos, sin) * (HEAD_DIM ** -0.5)
    k = _rope(x_ref[:, w:2 * w], cos, sin)
    v = x_ref[:, 2 * w:3 * w]
    qd = q * qdec_ref[...]
    kd = k * kdec_ref[...]
    g = x_ref[:, 3 * w:4 * w]
    per_seq = lambda a: a.reshape(nseq, c, HEAD_DIM)
    heads = range(N_HEADS)
    vb = [v[:, _head_slice(h)].astype(BF16) for h in heads]
    scores = [_dot_nt(q[:, _head_slice(h)].astype(BF16), k[:, _head_slice(h)].astype(BF16)) * dmat_ref[h]
              for h in heads]
    o_inner = [_dot(scores[h].astype(BF16), vb[h]) for h in heads]
    s0 = [s0_ref[:, h] for h in heads]
    o_cross = [_bmm('bqd,bde->bqe', per_seq(qd[:, _head_slice(h)]), s0[h]) for h in heads]
    kv = [_bmm('bkd,bke->bde', per_seq(kd[:, _head_slice(h)]), per_seq(v[:, _head_slice(h)])) for h in heads]
    for h in heads:
        hs = _head_slice(h)
        ns_ref[:, h] = gc_ref[:, hs] * s0[h] + kv[h]
        o = o_inner[h] + o_cross[h].reshape(nseq * c, HEAD_DIM)
        out_ref[:, hs] = _head_rms(o) * ng_ref[:, hs] * _silu(g[:, hs])


def _rope_tables(pos0, seq):
    inv = np.exp(-math.log(ROPE_BASE) * np.arange(HALF, dtype=np.float64) / HALF)
    ang = (pos0 + np.arange(seq, dtype=np.float64))[:, None] * inv[None]
    cos = np.tile(np.concatenate([np.cos(ang), np.cos(ang)], axis=1), (1, N_HEADS))
    sin = np.tile(np.concatenate([-np.sin(ang), np.sin(ang)], axis=1), (1, N_HEADS))
    return jnp.asarray(cos, dtype=F32), jnp.asarray(sin, dtype=F32)


def _ret_decay_tables(c):
    hh = np.arange(N_HEADS, dtype=np.float64)
    lg = np.log(1.0 - 2.0 ** (-5.0 - hh))
    n = np.arange(c, dtype=np.float64)
    rel = n[:, None] - n[None, :]
    dmat = np.where(rel[None] >= 0, np.exp(np.maximum(rel, 0.0)[None] * lg[:, None, None]), 0.0)
    rep = lambda a: np.repeat(a, HEAD_DIM, axis=-1)
    qdec = rep(np.exp((n + 1.0)[:, None] * lg[None, :]))
    kdec = rep(np.exp((c - 1.0 - n)[:, None] * lg[None, :]))
    gc = rep(np.exp(c * lg)[None, :])
    return dmat, qdec, kdec, gc


def _ret(x, ng, state, *, batch, seq, pos0):
    w = RET_WIDTH
    n = batch * seq
    c = math.gcd(seq, CHUNK)
    dmat, qdec, kdec, gc = _ret_decay_tables(c)
    cos, sin = _rope_tables(pos0, seq)
    f = lambda a: jnp.asarray(a, dtype=F32)
    if state is None:
        r = PROMPT_BLOCK
        nc = r // c
        nt = seq // r
        rows = lambda b, t: (b * nt + t, 0)
        trow = lambda b, t: (t, 0)
        const = lambda b, t: (0, 0)
        const3 = lambda b, t: (0, 0, 0)
        st = lambda b, t: (b, 0, 0, 0)
        out, ns = pl.pallas_call(
            functools.partial(_ret_prompt_kernel, r=r, c=c),
            grid=(batch, nt),
            in_specs=[pl.BlockSpec((r, RET_COLS), rows),
                      pl.BlockSpec((r, w), trow),
                      pl.BlockSpec((r, w), trow),
                      pl.BlockSpec((nc * N_HEADS, c, c), const3),
                      pl.BlockSpec((r, w), const),
                      pl.BlockSpec((r, w), const),
                      pl.BlockSpec((1, w), const),
                      pl.BlockSpec((1, w), const)],
            out_specs=[pl.BlockSpec((r, w), rows),
                       pl.BlockSpec((1, N_HEADS, HEAD_DIM, HEAD_DIM), st)],
            out_shape=[jax.ShapeDtypeStruct((n, w), F32),
                       jax.ShapeDtypeStruct((batch, N_HEADS, HEAD_DIM, HEAD_DIM), F32)],
            scratch_shapes=[pltpu.VMEM((N_HEADS, HEAD_DIM, HEAD_DIM), F32)],
            compiler_params=_cparams(("parallel", "arbitrary")),
            name="ret_prompt",
        )(x, cos, sin, f(np.tile(dmat, (nc, 1, 1))), f(np.tile(qdec, (nc, 1))), f(np.tile(kdec, (nc, 1))),
          f(gc), ng)
        return out, ns
    nseq = SAMPLE_SEQS
    r = nseq * c
    seg = np.arange(r) // c
    same = (seg[:, None] == seg[None, :])[None]
    dblk = np.where(same, np.tile(dmat, (1, nseq, nseq)), 0.0)
    rows = lambda i: (i, 0)
    const = lambda i: (0, 0)
    const3 = lambda i: (0, 0, 0)
    st = lambda i: (i, 0, 0, 0)
    sblk = (nseq, N_HEADS, HEAD_DIM, HEAD_DIM)
    out, ns = pl.pallas_call(
        functools.partial(_ret_sample_kernel, nseq=nseq, c=c),
        grid=(batch // nseq,),
        in_specs=[pl.BlockSpec((r, RET_COLS), rows),
                  pl.BlockSpec((r, w), const),
                  pl.BlockSpec((r, w), const),
                  pl.BlockSpec((N_HEADS, r, r), const3),
                  pl.BlockSpec((r, w), const),
                  pl.BlockSpec((r, w), const),
                  pl.BlockSpec((1, w), const),
                  pl.BlockSpec((1, w), const),
                  pl.BlockSpec(sblk, st)],
        out_specs=[pl.BlockSpec((r, w), rows), pl.BlockSpec(sblk, st)],
        out_shape=[jax.ShapeDtypeStruct((n, w), F32),
                   jax.ShapeDtypeStruct((batch, N_HEADS, HEAD_DIM, HEAD_DIM), F32)],
        compiler_params=_cparams(("parallel",)),
        name="ret_sample",
    )(x, jnp.tile(cos, (nseq, 1)), jnp.tile(sin, (nseq, 1)), f(dblk), f(np.tile(qdec, (nseq, 1))),
      f(np.tile(kdec, (nseq, 1))), f(gc), ng, state)
    return out, ns


def _unit_lower_inverse(a, eye, c):
    inv = eye - a
    x = a
    cover = 2
    while cover < c // 2:
        xm = x.astype(BF16)
        x = _bmm('bij,bjk->bik', xm, xm)
        inv = inv + _bmm('bij,bjk->bik', inv.astype(BF16), x.astype(BF16))
        cover *= 2
    resid = eye - inv - _bmm3('bij,bjk->bik', a, inv)
    return inv + _bmm('bij,bjk->bik', inv.astype(BF16), resid.astype(BF16))


def _gdn_gates(ab, alog_ref, dtb_ref, tri_l, tri_u):
    g_log = -jnp.exp(alog_ref[...]) * _softplus(ab + dtb_ref[...])
    beta = jax.nn.sigmoid(ab)
    hi = lax.Precision.HIGHEST
    g_col = _dot(tri_l.astype(F32), g_log, precision=hi)
    g_row = _dot_tn(g_log, tri_u.astype(F32), precision=hi)
    return g_col, g_row, beta


def _l2norm_heads(x, base):
    xs = x[:, base:base + GDN_WIDTH]
    xn = xs * lax.rsqrt(_head_sums(xs * xs) + EPS)
    return [xn[:, _head_slice(h)] for h in range(N_HEADS)]


def _gdn_prompt_kernel(x_ref, cw_ref, alog_ref, dtb_ref, ng_ref, out_ref, nconv_ref, ns_ref, ext, s_scr,
                       *, r, c):
    t = pl.program_id(1)
    last = pl.num_programs(1) - 1
    w = GDN_WIDTH
    hist = SUBLANES - (CONV_W - 1)
    nc = r // c

    @pl.when(t == 0)
    def _():
        ext[0:SUBLANES, :] = jnp.zeros((SUBLANES, 3 * w), F32)
        s_scr[...] = jnp.zeros(s_scr.shape, F32)

    qkv = _silu(_conv_unit(ext, x_ref[:, 0:3 * w], cw_ref, r))

    @pl.when(t == last)
    def _():
        nconv_ref[0] = ext[r + hist:r + SUBLANES, :]

    ext[0:SUBLANES, :] = ext[r:r + SUBLANES, :]
    tri_l, _, tri_u = _segment_masks(r, c)
    g_col, g_row, beta = _gdn_gates(x_ref[:, 4 * w:4 * w + LANES], alog_ref, dtb_ref, tri_l, tri_u)
    qn = [a * (HEAD_DIM ** -0.5) for a in _l2norm_heads(qkv, 0)]
    kn = _l2norm_heads(qkv, w)
    rows = lambda n: slice(n * c, (n + 1) * c)

    def stack(fn):
        return jnp.stack([fn(n, h) for n in range(nc) for h in range(N_HEADS)], axis=0)

    q = stack(lambda n, h: qn[h][rows(n)])
    k = stack(lambda n, h: kn[h][rows(n)])
    v = stack(lambda n, h: qkv[rows(n), _head_slice(h, 2 * w)])
    gi = stack(lambda n, h: g_col[rows(n), h:h + 1])
    gj = stack(lambda n, h: g_row[h:h + 1, rows(n)])
    bi = stack(lambda n, h: beta[rows(n), GDN_BETA_LANE + h:GDN_BETA_LANE + h + 1])
    gl = stack(lambda n, h: g_col[(n + 1) * c - 1:(n + 1) * c, h:h + 1])
    lower, strict, _ = _segment_masks(c, c)
    eye = (lower & jnp.logical_not(strict)).astype(F32)[None]
    decay = jnp.exp(jnp.where(lower[None], gi - gj, -jnp.inf))
    kb = k.astype(BF16)
    a_mat = jnp.where(strict[None], bi * _bmm('bid,bjd->bij', kb, kb) * decay, 0.0)
    tm = _unit_lower_inverse(a_mat, eye, c)
    eg = jnp.exp(gi)
    rhs = jnp.concatenate([bi * v, (bi * eg) * k], axis=2)
    uw = _bmm('bij,bjk->bik', tm.astype(BF16), rhs.astype(BF16))
    uu = uw[:, :, 0:HEAD_DIM]
    pm = (_bmm('bid,bjd->bij', q.astype(BF16), kb) * decay).astype(BF16)
    wq = jnp.concatenate([uw[:, :, HEAD_DIM:2 * HEAD_DIM], q * eg], axis=1).astype(BF16)
    kd = (k * jnp.exp(gl - gi)).astype(BF16)
    egl = jnp.exp(gl)
    z = x_ref[:, 3 * w:4 * w]
    s = s_scr[...]
    for n in range(nc):
        ps = slice(n * N_HEADS, (n + 1) * N_HEADS)
        wqs = _bmm('hid,hde->hie', wq[ps], s.astype(BF16))
        vn = (uu[ps] - wqs[:, 0:c]).astype(BF16)
        o = wqs[:, c:2 * c] + _bmm('hij,hje->hie', pm[ps], vn)
        s = egl[ps] * s + _bmm('hid,hie->hde', kd[ps], vn)
        on = _head_rms(o)
        for h in range(N_HEADS):
            hs = _head_slice(h)
            out_ref[rows(n), hs] = on[h] * ng_ref[:, hs] * _silu(z[rows(n), hs])
    s_scr[...] = s

    @pl.when(t == last)
    def _():
        ns_ref[0] = s


def _gdn_sample_kernel(x_ref, cw_ref, alog_ref, dtb_ref, ng_ref, buf_ref, s0_ref,
                       out_ref, nconv_ref, ns_ref, ext, y_scr, *, nseq, c):
    w = GDN_WIDTH
    hist = SUBLANES - (CONV_W - 1)
    r = nseq * c

    def conv_body(u, carry_):
        r0 = pl.multiple_of(u * c, SUBLANES)
        ext[hist:SUBLANES, :] = buf_ref[u]
        y_scr[pl.ds(r0, c), :] = _conv_unit(ext, x_ref[pl.ds(r0, c), 0:3 * w], cw_ref, c)
        nconv_ref[u] = ext[c + hist:c + SUBLANES, :]
        return carry_

    lax.fori_loop(0, nseq, conv_body, 0)
    qkv = _silu(y_scr[...])
    lower, strict, upper = _segment_masks(r, c)
    g_col, g_row, beta = _gdn_gates(x_ref[:, 4 * w:4 * w + LANES], alog_ref, dtb_ref, lower, upper)
    g_seq = g_col.reshape(nseq, c, LANES)
    g_end = g_seq[:, c - 1:c, :]
    g_end_rows = jnp.broadcast_to(g_end, (nseq, c, LANES)).reshape(r, LANES)
    qn = [a * (HEAD_DIM ** -0.5) for a in _l2norm_heads(qkv, 0)]
    kn = _l2norm_heads(qkv, w)
    heads = range(N_HEADS)
    hstack = lambda fn: jnp.stack([fn(h) for h in heads], axis=0)
    q = hstack(lambda h: qn[h])
    k = hstack(lambda h: kn[h])
    v = hstack(lambda h: qkv[:, _head_slice(h, 2 * w)])
    gi = hstack(lambda h: g_col[:, h:h + 1])
    gj = hstack(lambda h: g_row[h:h + 1, :])
    bi = hstack(lambda h: beta[:, GDN_BETA_LANE + h:GDN_BETA_LANE + h + 1])
    ge = hstack(lambda h: g_end_rows[:, h:h + 1])
    eye = (lower & jnp.logical_not(strict)).astype(F32)[None]
    decay = jnp.exp(jnp.where(lower[None], gi - gj, -jnp.inf))
    kb = k.astype(BF16)
    a_mat = jnp.where(strict[None], bi * _bmm('bid,bjd->bij', kb, kb) * decay, 0.0)
    tm = _unit_lower_inverse(a_mat, eye, c)
    eg = jnp.exp(gi)
    rhs = jnp.concatenate([bi * v, (bi * eg) * k], axis=2)
    uw = _bmm('bij,bjk->bik', tm.astype(BF16), rhs.astype(BF16))
    pm = (_bmm('bid,bjd->bij', q.astype(BF16), kb) * decay).astype(BF16)
    qg = q * eg
    kd = k * jnp.exp(ge - gi)
    per_seq = lambda a: a.reshape(nseq, c, HEAD_DIM)
    s0 = [s0_ref[:, h] for h in heads]
    wqs = [_bmm('bid,bde->bie',
                jnp.concatenate([per_seq(uw[h][:, HEAD_DIM:2 * HEAD_DIM]), per_seq(qg[h])], axis=1), s0[h])
           for h in heads]
    vn = [per_seq(uw[h][:, 0:HEAD_DIM]) - wqs[h][:, 0:c] for h in heads]
    o = [wqs[h][:, c:2 * c].reshape(r, HEAD_DIM)
         + _dot(pm[h], vn[h].reshape(r, HEAD_DIM).astype(BF16)) for h in heads]
    upd = [_bmm('bid,bie->bde', per_seq(kd[h]), vn[h]) for h in heads]
    z = x_ref[:, 3 * w:4 * w]
    for h in heads:
        hs = _head_slice(h)
        ns_ref[:, h] = jnp.exp(g_end[:, :, h:h + 1]) * s0[h] + upd[h]
        out_ref[:, hs] = _head_rms(o[h]) * ng_ref[:, hs] * _silu(z[:, hs])


def _gdn(x, cw, alog, dtb, ng, state, *, batch, seq):
    w = GDN_WIDTH
    n = batch * seq
    c = math.gcd(seq, CHUNK)
    if state is None:
        r = PROMPT_BLOCK
        nt = seq // r
        rows = lambda b, t: (b * nt + t, 0)
        const = lambda b, t: (0, 0)
        st3 = lambda b, t: (b, 0, 0)
        st4 = lambda b, t: (b, 0, 0, 0)
        return pl.pallas_call(
            functools.partial(_gdn_prompt_kernel, r=r, c=c),
            grid=(batch, nt),
            in_specs=[pl.BlockSpec((r, GDN_COLS), rows),
                      pl.BlockSpec((CONV_W, 3 * w), const),
                      pl.BlockSpec((1, LANES), const),
                      pl.BlockSpec((1, LANES), const),
                      pl.BlockSpec((1, w), const)],
            out_specs=[pl.BlockSpec((r, w), rows),
                       pl.BlockSpec((1, CONV_W - 1, 3 * w), st3),
                       pl.BlockSpec((1, N_HEADS, HEAD_DIM, HEAD_DIM), st4)],
            out_shape=[jax.ShapeDtypeStruct((n, w), F32),
                       jax.ShapeDtypeStruct((batch, CONV_W - 1, 3 * w), F32),
                       jax.ShapeDtypeStruct((batch, N_HEADS, HEAD_DIM, HEAD_DIM), F32)],
            scratch_shapes=[pltpu.VMEM((r + SUBLANES, 3 * w), F32),
                            pltpu.VMEM((N_HEADS, HEAD_DIM, HEAD_DIM), F32)],
            compiler_params=_cparams(("parallel", "arbitrary")),
            name="gdn_prompt",
        )(x, cw, alog, dtb, ng)
    nseq = SAMPLE_SEQS
    r = nseq * c
    rows = lambda i: (i, 0)
    const = lambda i: (0, 0)
    st3 = lambda i: (i, 0, 0)
    st4 = lambda i: (i, 0, 0, 0)
    cblk = (nseq, CONV_W - 1, 3 * w)
    sblk = (nseq, N_HEADS, HEAD_DIM, HEAD_DIM)
    return pl.pallas_call(
        functools.partial(_gdn_sample_kernel, nseq=nseq, c=c),
        grid=(batch // nseq,),
        in_specs=[pl.BlockSpec((r, GDN_COLS), rows),
                  pl.BlockSpec((CONV_W, 3 * w), const),
                  pl.BlockSpec((1, LANES), const),
                  pl.BlockSpec((1, LANES), const),
                  pl.BlockSpec((1, w), const),
                  pl.BlockSpec(cblk, st3),
                  pl.BlockSpec(sblk, st4)],
        out_specs=[pl.BlockSpec((r, w), rows), pl.BlockSpec(cblk, st3), pl.BlockSpec(sblk, st4)],
        out_shape=[jax.ShapeDtypeStruct((n, w), F32),
                   jax.ShapeDtypeStruct((batch, CONV_W - 1, 3 * w), F32),
                   jax.ShapeDtypeStruct((batch, N_HEADS, HEAD_DIM, HEAD_DIM), F32)],
        scratch_shapes=[pltpu.VMEM((c + SUBLANES, 3 * w), F32), pltpu.VMEM((r, 3 * w), F32)],
        compiler_params=_cparams(("parallel",)),
        name="gdn_sample",
    )(x, cw, alog, dtb, ng, *state)


def _block_diag(wb):
    out = jnp.zeros((LRU_WIDTH, LRU_WIDTH), wb.dtype)
    for n in range(LRU_BLOCKS):
        s = slice(n * LRU_BLOCK, (n + 1) * LRU_BLOCK)
        out = out.at[s, s].set(wb[n])
    return out


def _pad_lanes(v, offset=0):
    return jnp.zeros((1, LANES), F32).at[0, offset:offset + v.shape[0]].set(v.astype(F32))


def _prep_layer(W, l):
    row = lambda v: v.reshape(1, -1).astype(F32)
    pad = IN_COLS - W['w_in'].shape[2]
    wr = jnp.zeros((D_MODEL, LANES), F32)
    wr = wr.at[:, ROUTER_GROUP_LANE:ROUTER_GROUP_LANE + N_GROUPS].set(W['w_router_group'][l])
    wr = wr.at[:, ROUTER_EXPERT_LANE:ROUTER_EXPERT_LANE + N_EXPERTS].set(W['w_router_expert'][l])
    br = jnp.zeros((1, LANES), F32)
    br = br.at[0, ROUTER_GROUP_LANE:ROUTER_GROUP_LANE + N_GROUPS].set(W['b_router_group'][l])
    br = br.at[0, ROUTER_EXPERT_LANE:ROUTER_EXPERT_LANE + N_EXPERTS].set(W['b_router_expert'][l])
    return dict(
        norm_mix_g=row(W['norm_mix_g'][l]),
        w_in=jnp.pad(W['w_in'][l], ((0, 0), (0, pad))).astype(BF16),
        lru_conv_w=W['lru_conv_w'][l], lru_conv_b=row(W['lru_conv_b'][l]),
        lru_wa=_block_diag(W['lru_wa'][l]).astype(BF16), lru_ba=row(W['lru_ba'][l]),
        lru_wx=_block_diag(W['lru_wx'][l]).astype(BF16), lru_bx=row(W['lru_bx'][l]),
        lru_lambda=row(W['lru_lambda'][l]),
        ret_norm_g=row(W['ret_norm_g'][l]),
        gdn_conv_w=W['gdn_conv_w'][l],
        gdn_a_log=_pad_lanes(W['gdn_a_log'][l]), gdn_dt_bias=_pad_lanes(W['gdn_dt_bias'][l]),
        gdn_norm_g=jnp.tile(row(W['gdn_norm_g'][l]), (1, N_HEADS)),
        w_out=W['w_out'][l].astype(BF16),
        norm_ffn_g=row(W['norm_ffn_g'][l]),
        w_router=wr.astype(BF16), b_router=br,
        w_expert_gate=W['w_expert_gate'][l].astype(BF16),
        w_expert_up=W['w_expert_up'][l].astype(BF16),
        w_expert_down=W['w_expert_down'][l].astype(BF16),
        norm_pe_g=row(W['norm_pe_g'][l]),
        w_pe=W['w_pe'][l].astype(BF16),
        w_pe_gate=W['w_pe_gate'][l].astype(BF16),
    )


def _trunk(x, p, states, pos0, layers, final_g):
    batch, seq, _ = x.shape
    n = batch * seq
    h = x.reshape(n, D_MODEL)
    new = ([], [], [], [], [])
    for l, L in enumerate(layers):
        if states is None:
            st_lru = st_ret = st_gdn = None
        else:
            st_lru = (states[0][l], states[1][l])
            st_ret = states[2][l]
            st_gdn = (states[3][l], states[4][l])
        lru_x, ret_x, gdn_x = _in_proj(h, L['norm_mix_g'], L['w_in'])
        out_a, n_lru_conv, n_lru_h = _lru(
            lru_x, L['lru_conv_w'], L['lru_conv_b'], L['lru_wa'], L['lru_ba'], L['lru_wx'],
            L['lru_bx'], L['lru_lambda'], st_lru, batch=batch, seq=seq)
        out_b, n_ret = _ret(ret_x, L['ret_norm_g'], st_ret, batch=batch, seq=seq, pos0=pos0)
        out_c, n_gdn_conv, n_gdn = _gdn(
            gdn_x, L['gdn_conv_w'], L['gdn_a_log'], L['gdn_dt_bias'], L['gdn_norm_g'], st_gdn,
            batch=batch, seq=seq)
        h1, xn, gates = _out_router(h, out_a, out_b, out_c, L['w_out'], L['norm_ffn_g'],
                                    L['w_router'], L['b_router'])
        h2 = _moe(xn, gates, h1, L['w_expert_gate'], L['w_expert_up'], L['w_expert_down'])
        h = _pe(h2, p[l].reshape(n, PE_DIM), L['norm_pe_g'], L['w_pe_gate'], L['w_pe'], final_g,
                final=(l == len(layers) - 1))
        for lst, s in zip(new, (n_lru_conv, n_lru_h, n_ret, n_gdn_conv, n_gdn)):
            lst.append(s)
    return h.reshape(batch, seq, D_MODEL), tuple(jnp.stack(lst) for lst in new)


def kernel(x_prompt, x_sample, p_prompt, p_sample, state_lru_conv, state_lru_h, state_ret, state_gdn_conv, state_gdn, norm_mix_g, w_in, lru_conv_w, lru_conv_b, lru_wa, lru_ba, lru_wx, lru_bx, lru_lambda, ret_norm_g, gdn_conv_w, gdn_a_log, gdn_dt_bias, gdn_norm_g, w_out, norm_ffn_g, w_router_group, b_router_group, w_router_expert, b_router_expert, w_expert_gate, w_expert_up, w_expert_down, norm_pe_g, w_pe, w_pe_gate, final_norm_g):
    W = dict(norm_mix_g=norm_mix_g, w_in=w_in, lru_conv_w=lru_conv_w, lru_conv_b=lru_conv_b,
             lru_wa=lru_wa, lru_ba=lru_ba, lru_wx=lru_wx, lru_bx=lru_bx, lru_lambda=lru_lambda,
             ret_norm_g=ret_norm_g, gdn_conv_w=gdn_conv_w, gdn_a_log=gdn_a_log, gdn_dt_bias=gdn_dt_bias,
             gdn_norm_g=gdn_norm_g, w_out=w_out, norm_ffn_g=norm_ffn_g, w_router_group=w_router_group,
             b_router_group=b_router_group, w_router_expert=w_router_expert, b_router_expert=b_router_expert,
             w_expert_gate=w_expert_gate, w_expert_up=w_expert_up, w_expert_down=w_expert_down,
             norm_pe_g=norm_pe_g, w_pe=w_pe, w_pe_gate=w_pe_gate)
    layers = [_prep_layer(W, l) for l in range(DEPTH)]
    final_g = final_norm_g.reshape(1, D_MODEL).astype(F32)
    y_p, st_p = _trunk(x_prompt, p_prompt, None, 0, layers, final_g)
    sample_states = (state_lru_conv, state_lru_h, state_ret, state_gdn_conv, state_gdn)
    y_s, st_s = _trunk(x_sample, p_sample, sample_states, PAST_LEN, layers, final_g)
    return (y_p, y_s) + st_p + st_s
```

```python
import functools
import math

import jax
import jax.numpy as jnp
import numpy as np
from jax import lax
from jax.experimental import pallas as pl
from jax.experimental.pallas import tpu as pltpu

F32 = jnp.float32
BF16 = jnp.bfloat16

D_MODEL = 1024
DEPTH = 2
PAST_LEN = 16384
HEAD_DIM = 64
HALF = HEAD_DIM // 2
LRU_WIDTH = 256
LRU_BLOCKS = 4
LRU_BLOCK = 64
LRU_C = 8.0
RET_WIDTH = 384
GDN_WIDTH = 384
N_HEADS = 6
CONV_W = 4
CHUNK = 64
ROPE_BASE = 10000.0
PE_DIM = 256
N_GROUPS = 4
EXPERTS_PER_GROUP = 4
N_EXPERTS = 16
D_EXPERT = 256
EPS = 1e-6

LANES = 128
SUBLANES = 8
VMEM_LIMIT = 48 * 1024 * 1024

LRU_COLS = 2 * LRU_WIDTH
RET_COLS = 4 * RET_WIDTH
GDN_COLS = 4 * GDN_WIDTH + LANES
IN_COLS = LRU_COLS + RET_COLS + GDN_COLS
ROUTER_GROUP_LANE = 0
ROUTER_EXPERT_LANE = N_GROUPS
GDN_BETA_LANE = N_HEADS

D_GROUP = EXPERTS_PER_GROUP * D_EXPERT

TOKEN_TILE = 512
MOE_TILE = 512
PROMPT_BLOCK = 256
SAMPLE_SEQS = 16


def _cparams(sem):
    return pltpu.CompilerParams(dimension_semantics=sem, vmem_limit_bytes=VMEM_LIMIT)


def _rmsnorm(x, g):
    return x * lax.rsqrt(jnp.mean(x * x, axis=-1, keepdims=True) + EPS) * g


def _dot(a, b, **kw):
    return jnp.dot(a, b, preferred_element_type=F32, **kw)


def _dot_nt(a, b):
    return lax.dot_general(a, b, (((1,), (1,)), ((), ())), preferred_element_type=F32)


def _dot_tn(a, b, **kw):
    return lax.dot_general(a, b, (((0,), (0,)), ((), ())), preferred_element_type=F32, **kw)


def _bmm(spec, a, b):
    return jnp.einsum(spec, a, b, preferred_element_type=F32)


def _bmm3(spec, a, b):
    a_hi = a.astype(BF16)
    a_lo = (a - a_hi.astype(F32)).astype(BF16)
    b_hi = b.astype(BF16)
    b_lo = (b - b_hi.astype(F32)).astype(BF16)
    return _bmm(spec, a_hi, b_hi) + (_bmm(spec, a_hi, b_lo) + _bmm(spec, a_lo, b_hi))


def _softplus(x):
    return jnp.maximum(x, 0.0) + jnp.log1p(jnp.exp(-jnp.abs(x)))


def _silu(x):
    return x * jax.nn.sigmoid(x)


def _gelu_tanh(x):
    return 0.5 * x * (1.0 + jnp.tanh(math.sqrt(2.0 / math.pi) * (x + 0.044715 * (x * x * x))))


def _in_proj_kernel(h_ref, g_ref, w_ref, lru_ref, ret_ref, gdn_ref):
    xn = _rmsnorm(h_ref[...], g_ref[...]).astype(BF16)
    lru_ref[...] = _dot(xn, w_ref[:, 0:LRU_COLS])
    ret_ref[...] = _dot(xn, w_ref[:, LRU_COLS:LRU_COLS + RET_COLS])
    gdn_ref[...] = _dot(xn, w_ref[:, LRU_COLS + RET_COLS:IN_COLS])


def _in_proj(h, g, w):
    n = h.shape[0]
    tm = TOKEN_TILE
    row = lambda i: (i, 0)
    const = lambda i: (0, 0)
    return pl.pallas_call(
        _in_proj_kernel,
        grid=(n // tm,),
        in_specs=[pl.BlockSpec((tm, D_MODEL), row),
                  pl.BlockSpec((1, D_MODEL), const),
                  pl.BlockSpec((D_MODEL, IN_COLS), const)],
        out_specs=[pl.BlockSpec((tm, LRU_COLS), row),
                   pl.BlockSpec((tm, RET_COLS), row),
                   pl.BlockSpec((tm, GDN_COLS), row)],
        out_shape=[jax.ShapeDtypeStruct((n, LRU_COLS), F32),
                   jax.ShapeDtypeStruct((n, RET_COLS), F32),
                   jax.ShapeDtypeStruct((n, GDN_COLS), F32)],
        compiler_params=_cparams(("parallel",)),
        name="in_proj",
    )(h, g, w)


def _route(logits):
    lane = lax.broadcasted_iota(jnp.int32, logits.shape, 1).astype(F32)
    neg = -jnp.inf
    far = float(LANES)
    gmask = lane < N_GROUPS
    gl = jnp.where(gmask, logits, neg)
    gmax = jnp.max(gl, axis=-1, keepdims=True)
    gsel = jnp.min(jnp.where(gl == gmax, lane, far), axis=-1, keepdims=True)
    gsum = jnp.sum(jnp.where(gmask, jnp.exp(logits - gmax), 0.0), axis=-1, keepdims=True)
    g_w = 1.0 / gsum
    lo = ROUTER_EXPERT_LANE + EXPERTS_PER_GROUP * gsel
    el = jnp.where(lane >= lo, jnp.where(lane < lo + EXPERTS_PER_GROUP, logits, neg), neg)
    m1 = jnp.max(el, axis=-1, keepdims=True)
    i1 = jnp.min(jnp.where(el == m1, lane, far), axis=-1, keepdims=True)
    el2 = jnp.where(lane == i1, neg, el)
    m2 = jnp.max(el2, axis=-1, keepdims=True)
    i2 = jnp.min(jnp.where(el2 == m2, lane, far), axis=-1, keepdims=True)
    e2 = jnp.exp(m2 - m1)
    w1 = g_w / (1.0 + e2)
    w2 = w1 * e2
    return jnp.where(lane == i1, w1, 0.0) + jnp.where(lane == i2, w2, 0.0)


def _ffn_kernel(h_ref, a_ref, b_ref, c_ref, wo_ref, g_ref, wr_ref, br_ref, wg_ref, wu_ref, wd_ref,
                out_ref, xn_scr, gates_scr):
    grp = pl.program_id(1)

    @pl.when(grp == 0)
    def _():
        o1 = LRU_WIDTH
        o2 = LRU_WIDTH + RET_WIDTH
        mix = (_dot(a_ref[...].astype(BF16), wo_ref[0:o1, :])
               + _dot(b_ref[...].astype(BF16), wo_ref[o1:o2, :])
               + _dot(c_ref[...].astype(BF16), wo_ref[o2:D_MODEL, :]))
        h1 = h_ref[...] + mix
        out_ref[...] = h1
        xn0 = _rmsnorm(h1, g_ref[...]).astype(BF16)
        xn_scr[...] = xn0
        gates_scr[...] = _route(_dot(xn0, wr_ref[...]) + br_ref[...])

    xn = xn_scr[...]
    gates = gates_scr[...]
    lane = lax.broadcasted_iota(jnp.int32, gates.shape, 1)
    base = ROUTER_EXPERT_LANE + EXPERTS_PER_GROUP * grp
    parts = []
    for j in range(EXPERTS_PER_GROUP):
        hg = _dot(xn, wg_ref[j])
        hu = _dot(xn, wu_ref[j])
        gcol = jnp.sum(jnp.where(lane == base + j, gates, 0.0), axis=-1, keepdims=True)
        parts.append((_silu(hg) * hu * gcol).astype(BF16))
    out_ref[...] += _dot(jnp.concatenate(parts, axis=1), wd_ref[...].reshape(D_GROUP, D_MODEL))


def _ffn(h, a, b, c, wo, g, wr, br, wg, wu, wd):
    n = h.shape[0]
    tm = MOE_TILE
    row = lambda i, e: (i, 0)
    const = lambda i, e: (0, 0)
    grp = lambda i, e: (e, 0, 0)
    return pl.pallas_call(
        _ffn_kernel,
        grid=(n // tm, N_GROUPS),
        in_specs=[pl.BlockSpec((tm, D_MODEL), row),
                  pl.BlockSpec((tm, LRU_WIDTH), row),
                  pl.BlockSpec((tm, RET_WIDTH), row),
                  pl.BlockSpec((tm, GDN_WIDTH), row),
                  pl.BlockSpec((D_MODEL, D_MODEL), const),
                  pl.BlockSpec((1, D_MODEL), const),
                  pl.BlockSpec((D_MODEL, LANES), const),
                  pl.BlockSpec((1, LANES), const),
                  pl.BlockSpec((EXPERTS_PER_GROUP, D_MODEL, D_EXPERT), grp),
                  pl.BlockSpec((EXPERTS_PER_GROUP, D_MODEL, D_EXPERT), grp),
                  pl.BlockSpec((EXPERTS_PER_GROUP, D_EXPERT, D_MODEL), grp)],
        out_specs=pl.BlockSpec((tm, D_MODEL), row),
        out_shape=jax.ShapeDtypeStruct((n, D_MODEL), F32),
        scratch_shapes=[pltpu.VMEM((tm, D_MODEL), BF16), pltpu.VMEM((tm, LANES), F32)],
        compiler_params=_cparams(("parallel", "arbitrary")),
        name="ffn",
    )(h, a, b, c, wo, g, wr, br, wg, wu, wd)


def _pe_kernel(h_ref, p_ref, g_ref, wgate_ref, wpe_ref, fg_ref, out_ref, *, final):
    h = h_ref[...]
    xn = _rmsnorm(h, g_ref[...]).astype(BF16)
    gate = jax.nn.sigmoid(_dot(xn, wgate_ref[...]))
    pe = _dot(p_ref[...].astype(BF16), wpe_ref[...])
    h3 = h + pe * gate
    out_ref[...] = _rmsnorm(h3, fg_ref[...]) if final else h3


def _pe(h, p, g, wgate, wpe, fg, final):
    n = h.shape[0]
    tm = TOKEN_TILE
    row = lambda i: (i, 0)
    const = lambda i: (0, 0)
    return pl.pallas_call(
        functools.partial(_pe_kernel, final=final),
        grid=(n // tm,),
        in_specs=[pl.BlockSpec((tm, D_MODEL), row),
                  pl.BlockSpec((tm, PE_DIM), row),
                  pl.BlockSpec((1, D_MODEL), const),
                  pl.BlockSpec((D_MODEL, D_MODEL), const),
                  pl.BlockSpec((PE_DIM, D_MODEL), const),
                  pl.BlockSpec((1, D_MODEL), const)],
        out_specs=pl.BlockSpec((tm, D_MODEL), row),
        out_shape=jax.ShapeDtypeStruct((n, D_MODEL), F32),
        compiler_params=_cparams(("parallel",)),
        name="pe_gate",
    )(h, p, g, wgate, wpe, fg)


def _conv_unit(ext, x, w_ref, c):
    ext[SUBLANES:SUBLANES + c, :] = x
    y = w_ref[CONV_W - 1:CONV_W, :] * x
    for j in range(CONV_W - 1):
        off = SUBLANES - (CONV_W - 1) + j
        y = y + w_ref[j:j + 1, :] * ext[off:off + c, :]
    return y


def _lru_kernel(*refs, c, units, carry):
    if carry:
        (x_ref, cw_ref, cb_ref, wa_ref, ba_ref, wx_ref, bx_ref, lam_ref,
         out_ref, nconv_ref, nh_ref, ext, y_scr, a_scr, b_scr, hcar) = refs
    else:
        (x_ref, cw_ref, cb_ref, wa_ref, ba_ref, wx_ref, bx_ref, lam_ref, buf_ref, h0_ref,
         out_ref, nconv_ref, nh_ref, ext, y_scr, a_scr, b_scr) = refs
    w = LRU_WIDTH
    hist = SUBLANES - (CONV_W - 1)
    if carry:
        t = pl.program_id(1)
        last = pl.num_programs(1) - 1

        @pl.when(t == 0)
        def _():
            ext[0:SUBLANES, :] = jnp.zeros((SUBLANES, w), F32)
            hcar[...] = jnp.zeros((1, w), F32)

    def conv_body(u, carry_):
        r0 = pl.multiple_of(u * c, SUBLANES)
        if not carry:
            ext[hist:SUBLANES, :] = buf_ref[u]
        x = x_ref[pl.ds(r0, c), 0:w]
        y_scr[pl.ds(r0, c), :] = _conv_unit(ext, x, cw_ref, c) + cb_ref[...]
        if carry:
            @pl.when(t == last)
            def _():
                nconv_ref[0] = ext[c + hist:c + SUBLANES, :]
            ext[0:SUBLANES, :] = ext[c:c + SUBLANES, :]
        else:
            nconv_ref[u] = ext[c + hist:c + SUBLANES, :]
        return carry_

    lax.fori_loop(0, units, conv_body, 0)

    y = y_scr[...]
    yb = y.astype(BF16)
    r = jax.nn.sigmoid(_dot(yb, wa_ref[...]) + ba_ref[...])
    i = jax.nn.sigmoid(_dot(yb, wx_ref[...]) + bx_ref[...])
    log_a = (-LRU_C) * r * _softplus(-lam_ref[...])
    th = jnp.tanh(log_a)
    a_scr[...] = jnp.exp(log_a)
    b_scr[...] = jnp.sqrt(-2.0 * th / (1.0 - th)) * (i * y)

    def scan_body(u, carry_):
        r0 = pl.multiple_of(u * c, SUBLANES)
        a = a_scr[pl.ds(r0, c), :]
        b = b_scr[pl.ds(r0, c), :]
        rowi = lax.broadcasted_iota(jnp.int32, (c, w), 0)
        d = 1
        while d < c:
            keep = rowi >= d
            a_s = pltpu.roll(a, d, axis=0)
            b_s = pltpu.roll(b, d, axis=0)
            b = jnp.where(keep, a * b_s + b, b)
            a = jnp.where(keep, a * a_s, a)
            d *= 2
        h0 = hcar[...] if carry else h0_ref[u]
        h = a * h0 + b
        gate = x_ref[pl.ds(r0, c), w:2 * w]
        out_ref[pl.ds(r0, c), :] = h * _gelu_tanh(gate)
        if carry:
            hcar[...] = h[c - 1:c, :]
        else:
            nh_ref[u] = h[c - 1:c, :]
        return carry_

    lax.fori_loop(0, units, scan_body, 0)
    if carry:
        @pl.when(t == last)
        def _():
            nh_ref[0] = hcar[...]


def _lru(x, cw, cb, wa, ba, wx, bx, lam, state, *, batch, seq):
    w = LRU_WIDTH
    carry = state is None
    if carry:
        c, units = PROMPT_BLOCK, 1
        nt = seq // c
        grid = (batch, nt)
        rows = lambda b, t: (b * nt + t, 0)
        const = lambda b, t: (0, 0)
        st = lambda b, t: (b, 0, 0)
        sem = ("parallel", "arbitrary")
        state_in, state_specs = [], []
    else:
        c, units = seq, batch
        grid = (1,)
        rows = lambda i: (0, 0)
        const = lambda i: (0, 0)
        st = lambda i: (0, 0, 0)
        sem = ("arbitrary",)
        state_in = [state[0], state[1].reshape(batch, 1, w)]
        state_specs = [pl.BlockSpec((units, CONV_W - 1, w), st), pl.BlockSpec((units, 1, w), st)]
    n = batch * seq
    blk = c * units
    scratch = [pltpu.VMEM((c + SUBLANES, w), F32), pltpu.VMEM((blk, w), F32),
               pltpu.VMEM((blk, w), F32), pltpu.VMEM((blk, w), F32)]
    if carry:
        scratch.append(pltpu.VMEM((1, w), F32))
    out, nconv, nh = pl.pallas_call(
        functools.partial(_lru_kernel, c=c, units=units, carry=carry),
        grid=grid,
        in_specs=[pl.BlockSpec((blk, LRU_COLS), rows),
                  pl.BlockSpec((CONV_W, w), const),
                  pl.BlockSpec((1, w), const),
                  pl.BlockSpec((w, w), const),
                  pl.BlockSpec((1, w), const),
                  pl.BlockSpec((w, w), const),
                  pl.BlockSpec((1, w), const),
                  pl.BlockSpec((1, w), const)] + state_specs,
        out_specs=[pl.BlockSpec((blk, w), rows),
                   pl.BlockSpec((units, CONV_W - 1, w), st),
                   pl.BlockSpec((units, 1, w), st)],
        out_shape=[jax.ShapeDtypeStruct((n, w), F32),
                   jax.ShapeDtypeStruct((batch, CONV_W - 1, w), F32),
                   jax.ShapeDtypeStruct((batch, 1, w), F32)],
        scratch_shapes=scratch,
        compiler_params=_cparams(sem),
        name="lru_prompt" if carry else "lru_sample",
    )(x, cw, cb, wa, ba, wx, bx, lam, *state_in)
    return out, nconv, nh.reshape(batch, w)


def _rope(x, cos, sin):
    parts = []
    for j in range(x.shape[1] // LANES):
        sl = slice(j * LANES, (j + 1) * LANES)
        xs = x[:, sl]
        lane = lax.broadcasted_iota(jnp.int32, xs.shape, 1)
        swapped = jnp.where((lane & HALF) == 0,
                            pltpu.roll(xs, LANES - HALF, axis=1), pltpu.roll(xs, HALF, axis=1))
        parts.append(xs * cos[:, sl] + swapped * sin[:, sl])
    return jnp.concatenate(parts, axis=1)


def _segment_masks(r, c):
    ri = lax.broadcasted_iota(jnp.int32, (r, r), 0)
    ci = lax.broadcasted_iota(jnp.int32, (r, r), 1)
    if c == r:
        return ri >= ci, ri > ci, ri <= ci
    shift = c.bit_length() - 1
    same = lax.shift_right_logical(ri, shift) == lax.shift_right_logical(ci, shift)
    return same & (ri >= ci), same & (ri > ci), same & (ri <= ci)


def _head_slice(h, base=0):
    return slice(base + h * HEAD_DIM, base + (h + 1) * HEAD_DIM)


def _head_rms(o):
    return o * lax.rsqrt(jnp.mean(o * o, axis=-1, keepdims=True) + EPS)


def _head_sums(x):
    wdt = x.shape[1]
    shift = HEAD_DIM.bit_length() - 1
    same = (lax.shift_right_logical(lax.broadcasted_iota(jnp.int32, (wdt, wdt), 0), shift)
            == lax.shift_right_logical(lax.broadcasted_iota(jnp.int32, (wdt, wdt), 1), shift))
    ones = jnp.where(same, 1.0, 0.0).astype(BF16)
    hi = x.astype(BF16)
    lo = (x - hi.astype(F32)).astype(BF16)
    return _dot(hi, ones) + _dot(lo, ones)


def _ret_prompt_kernel(x_ref, cos_ref, sin_ref, dmat_ref, qdec_ref, kdec_ref, gc_ref, ng_ref,
                       out_ref, ns_ref, s_scr, *, r, c):
    t = pl.program_id(1)
    last = pl.num_programs(1) - 1

    @pl.when(t == 0)
    def _():
        s_scr[...] = jnp.zeros(s_scr.shape, F32)

    w = RET_WIDTH
    nc = r // c
    cos = cos_ref[...]
    sin = sin_ref[...]
    q = _rope(x_ref[:, 0:w], cos, sin) * (HEAD_DIM ** -0.5)
    k = _rope(x_ref[:, w:2 * w], cos, sin)
    v = x_ref[:, 2 * w:3 * w]
    qd = q * qdec_ref[...]
    kd = k * kdec_ref[...]
    rows = lambda n: slice(n * c, (n + 1) * c)

    def stack(a):
        return jnp.stack([a[rows(n), _head_slice(h)] for n in range(nc) for h in range(N_HEADS)], axis=0)

    qs, ks, vs = stack(q).astype(BF16), stack(k).astype(BF16), stack(v).astype(BF16)
    scores = _bmm('bid,bjd->bij', qs, ks) * dmat_ref[...]
    o_inner = _bmm('bij,bje->bie', scores.astype(BF16), vs)
    kv = _bmm('bid,bie->bde', stack(kd).astype(BF16), vs)
    gc = jnp.stack([gc_ref[:, _head_slice(h)] for h in range(N_HEADS)], axis=0)
    s = s_scr[...]
    starts = []
    for n in range(nc):
        starts.append(s)
        s = gc * s + kv[n * N_HEADS:(n + 1) * N_HEADS]
    s_scr[...] = s

    @pl.when(t == last)
    def _():
        ns_ref[0] = s

    o = o_inner + _bmm('bid,bde->bie', stack(qd).astype(BF16), jnp.concatenate(starts, axis=0).astype(BF16))
    on = _head_rms(o)
    g = x_ref[:, 3 * w:4 * w]
    for n in range(nc):
        for h in range(N_HEADS):
            hs = _head_slice(h)
            out_ref[rows(n), hs] = on[n * N_HEADS + h] * ng_ref[:, hs] * _silu(g[rows(n), hs])


def _ret_sample_kernel(x_ref, cos_ref, sin_ref, dmat_ref, qdec_ref, kdec_ref, gc_ref, ng_ref, s0_ref,
                       out_ref, ns_ref, *, nseq, c):
    w = RET_WIDTH
    cos = cos_ref[...]
    sin = sin_ref[...]
    q = _rope(x_ref[:, 0:w], cos, sin) * (HEAD_DIM ** -0.5)
    k = _rope(x_ref[:, w:2 * w], cos, sin)
    v = x_ref[:, 2 * w:3 * w]
    qd = q * qdec_ref[...]
    kd = k * kdec_ref[...]
    g = x_ref[:, 3 * w:4 * w]
    per_seq = lambda a: a.reshape(nseq, c, HEAD_DIM)
    heads = range(N_HEADS)
    vb = [v[:, _head_slice(h)].astype(BF16) for h in heads]
    scores = [_dot_nt(q[:, _head_slice(h)].astype(BF16), k[:, _head_slice(h)].astype(BF16)) * dmat_ref[h]
              for h in heads]
    o_inner = [_dot(scores[h].astype(BF16), vb[h]) for h in heads]
    s0 = [s0_ref[:, h] for h in heads]
    o_cross = [_bmm('bqd,bde->bqe', per_seq(qd[:, _head_slice(h)]), s0[h]) for h in heads]
    kv = [_bmm('bkd,bke->bde', per_seq(kd[:, _head_slice(h)]), per_seq(v[:, _head_slice(h)])) for h in heads]
    for h in heads:
        hs = _head_slice(h)
        ns_ref[:, h] = gc_ref[:, hs] * s0[h] + kv[h]
        o = o_inner[h] + o_cross[h].reshape(nseq * c, HEAD_DIM)
        out_ref[:, hs] = _head_rms(o) * ng_ref[:, hs] * _silu(g[:, hs])


def _rope_tables(pos0, seq):
    inv = np.exp(-math.log(ROPE_BASE) * np.arange(HALF, dtype=np.float64) / HALF)
    ang = (pos0 + np.arange(seq, dtype=np.float64))[:, None] * inv[None]
    cos = np.tile(np.concatenate([np.cos(ang), np.cos(ang)], axis=1), (1, N_HEADS))
    sin = np.tile(np.concatenate([-np.sin(ang), np.sin(ang)], axis=1), (1, N_HEADS))
    return jnp.asarray(cos, dtype=F32), jnp.asarray(sin, dtype=F32)


def _ret_decay_tables(c):
    hh = np.arange(N_HEADS, dtype=np.float64)
    lg = np.log(1.0 - 2.0 ** (-5.0 - hh))
    n = np.arange(c, dtype=np.float64)
    rel = n[:, None] - n[None, :]
    dmat = np.where(rel[None] >= 0, np.exp(np.maximum(rel, 0.0)[None] * lg[:, None, None]), 0.0)
    rep = lambda a: np.repeat(a, HEAD_DIM, axis=-1)
    qdec = rep(np.exp((n + 1.0)[:, None] * lg[None, :]))
    kdec = rep(np.exp((c - 1.0 - n)[:, None] * lg[None, :]))
    gc = rep(np.exp(c * lg)[None, :])
    return dmat, qdec, kdec, gc


def _ret(x, ng, state, *, batch, seq, pos0):
    w = RET_WIDTH
    n = batch * seq
    c = math.gcd(seq, CHUNK)
    dmat, qdec, kdec, gc = _ret_decay_tables(c)
    cos, sin = _rope_tables(pos0, seq)
    f = lambda a: jnp.asarray(a, dtype=F32)
    if state is None:
        r = PROMPT_BLOCK
        nc = r // c
        nt = seq // r
        rows = lambda b, t: (b * nt + t, 0)
        trow = lambda b, t: (t, 0)
        const = lambda b, t: (0, 0)
        const3 = lambda b, t: (0, 0, 0)
        st = lambda b, t: (b, 0, 0, 0)
        out, ns = pl.pallas_call(
            functools.partial(_ret_prompt_kernel, r=r, c=c),
            grid=(batch, nt),
            in_specs=[pl.BlockSpec((r, RET_COLS), rows),
                      pl.BlockSpec((r, w), trow),
                      pl.BlockSpec((r, w), trow),
                      pl.BlockSpec((nc * N_HEADS, c, c), const3),
                      pl.BlockSpec((r, w), const),
                      pl.BlockSpec((r, w), const),
                      pl.BlockSpec((1, w), const),
                      pl.BlockSpec((1, w), const)],
            out_specs=[pl.BlockSpec((r, w), rows),
                       pl.BlockSpec((1, N_HEADS, HEAD_DIM, HEAD_DIM), st)],
            out_shape=[jax.ShapeDtypeStruct((n, w), F32),
                       jax.ShapeDtypeStruct((batch, N_HEADS, HEAD_DIM, HEAD_DIM), F32)],
            scratch_shapes=[pltpu.VMEM((N_HEADS, HEAD_DIM, HEAD_DIM), F32)],
            compiler_params=_cparams(("parallel", "arbitrary")),
            name="ret_prompt",
        )(x, cos, sin, f(np.tile(dmat, (nc, 1, 1))), f(np.tile(qdec, (nc, 1))), f(np.tile(kdec, (nc, 1))),
          f(gc), ng)
        return out, ns
    nseq = SAMPLE_SEQS
    r = nseq * c
    seg = np.arange(r) // c
    same = (seg[:, None] == seg[None, :])[None]
    dblk = np.where(same, np.tile(dmat, (1, nseq, nseq)), 0.0)
    rows = lambda i: (i, 0)
    const = lambda i: (0, 0)
    const3 = lambda i: (0, 0, 0)
    st = lambda i: (i, 0, 0, 0)
    sblk = (nseq, N_HEADS, HEAD_DIM, HEAD_DIM)
    out, ns = pl.pallas_call(
        functools.partial(_ret_sample_kernel, nseq=nseq, c=c),
        grid=(batch // nseq,),
        in_specs=[pl.BlockSpec((r, RET_COLS), rows),
                  pl.BlockSpec((r, w), const),
                  pl.BlockSpec((r, w), const),
                  pl.BlockSpec((N_HEADS, r, r), const3),
                  pl.BlockSpec((r, w), const),
                  pl.BlockSpec((r, w), const),
                  pl.BlockSpec((1, w), const),
                  pl.BlockSpec((1, w), const),
                  pl.BlockSpec(sblk, st)],
        out_specs=[pl.BlockSpec((r, w), rows), pl.BlockSpec(sblk, st)],
        out_shape=[jax.ShapeDtypeStruct((n, w), F32),
                   jax.ShapeDtypeStruct((batch, N_HEADS, HEAD_DIM, HEAD_DIM), F32)],
        compiler_params=_cparams(("parallel",)),
        name="ret_sample",
    )(x, jnp.tile(cos, (nseq, 1)), jnp.tile(sin, (nseq, 1)), f(dblk), f(np.tile(qdec, (nseq, 1))),
      f(np.tile(kdec, (nseq, 1))), f(gc), ng, state)
    return out, ns


def _unit_lower_inverse(a, eye, c):
    inv = eye - a
    x = a
    cover = 2
    while cover < c // 2:
        xm = x.astype(BF16)
        x = _bmm('bij,bjk->bik', xm, xm)
        inv = inv + _bmm('bij,bjk->bik', inv.astype(BF16), x.astype(BF16))
        cover *= 2
    resid = eye - inv - _bmm3('bij,bjk->bik', a, inv)
    return inv + _bmm('bij,bjk->bik', inv.astype(BF16), resid.astype(BF16))


def _gdn_gates(ab, alog_ref, dtb_ref, tri_l, tri_u):
    g_log = -jnp.exp(alog_ref[...]) * _softplus(ab + dtb_ref[...])
    beta = jax.nn.sigmoid(ab)
    hi = lax.Precision.HIGHEST
    g_col = _dot(tri_l.astype(F32), g_log, precision=hi)
    g_row = _dot_tn(g_log, tri_u.astype(F32), precision=hi)
    return g_col, g_row, beta


def _l2norm_heads(x, base):
    xs = x[:, base:base + GDN_WIDTH]
    xn = xs * lax.rsqrt(_head_sums(xs * xs) + EPS)
    return [xn[:, _head_slice(h)] for h in range(N_HEADS)]


def _gdn_prompt_kernel(x_ref, cw_ref, alog_ref, dtb_ref, ng_ref, out_ref, nconv_ref, ns_ref, ext, s_scr,
                       *, r, c):
    t = pl.program_id(1)
    last = pl.num_programs(1) - 1
    w = GDN_WIDTH
    hist = SUBLANES - (CONV_W - 1)
    nc = r // c

    @pl.when(t == 0)
    def _():
        ext[0:SUBLANES, :] = jnp.zeros((SUBLANES, 3 * w), F32)
        s_scr[...] = jnp.zeros(s_scr.shape, F32)

    qkv = _silu(_conv_unit(ext, x_ref[:, 0:3 * w], cw_ref, r))

    @pl.when(t == last)
    def _():
        nconv_ref[0] = ext[r + hist:r + SUBLANES, :]

    ext[0:SUBLANES, :] = ext[r:r + SUBLANES, :]
    tri_l, _, tri_u = _segment_masks(r, c)
    g_col, g_row, beta = _gdn_gates(x_ref[:, 4 * w:4 * w + LANES], alog_ref, dtb_ref, tri_l, tri_u)
    qn = [a * (HEAD_DIM ** -0.5) for a in _l2norm_heads(qkv, 0)]
    kn = _l2norm_heads(qkv, w)
    rows = lambda n: slice(n * c, (n + 1) * c)

    def stack(fn):
        return jnp.stack([fn(n, h) for n in range(nc) for h in range(N_HEADS)], axis=0)

    q = stack(lambda n, h: qn[h][rows(n)])
    k = stack(lambda n, h: kn[h][rows(n)])
    v = stack(lambda n, h: qkv[rows(n), _head_slice(h, 2 * w)])
    gi = stack(lambda n, h: g_col[rows(n), h:h + 1])
    gj = stack(lambda n, h: g_row[h:h + 1, rows(n)])
    bi = stack(lambda n, h: beta[rows(n), GDN_BETA_LANE + h:GDN_BETA_LANE + h + 1])
    gl = stack(lambda n, h: g_col[(n + 1) * c - 1:(n + 1) * c, h:h + 1])
    lower, strict, _ = _segment_masks(c, c)
    eye = (lower & jnp.logical_not(strict)).astype(F32)[None]
    decay = jnp.exp(jnp.where(lower[None], gi - gj, -jnp.inf))
    kb = k.astype(BF16)
    a_mat = jnp.where(strict[None], bi * _bmm('bid,bjd->bij', kb, kb) * decay, 0.0)
    tm = _unit_lower_inverse(a_mat, eye, c)
    eg = jnp.exp(gi)
    rhs = jnp.concatenate([bi * v, (bi * eg) * k], axis=2)
    uw = _bmm('bij,bjk->bik', tm.astype(BF16), rhs.astype(BF16))
    uu = uw[:, :, 0:HEAD_DIM]
    pm = (_bmm('bid,bjd->bij', q.astype(BF16), kb) * decay).astype(BF16)
    wq = jnp.concatenate([uw[:, :, HEAD_DIM:2 * HEAD_DIM], q * eg], axis=1).astype(BF16)
    kd = (k * jnp.exp(gl - gi)).astype(BF16)
    egl = jnp.exp(gl)
    z = x_ref[:, 3 * w:4 * w]
    s = s_scr[...]
    for n in range(nc):
        ps = slice(n * N_HEADS, (n + 1) * N_HEADS)
        wqs = _bmm('hid,hde->hie', wq[ps], s.astype(BF16))
        vn = (uu[ps] - wqs[:, 0:c]).astype(BF16)
        o = wqs[:, c:2 * c] + _bmm('hij,hje->hie', pm[ps], vn)
        s = egl[ps] * s + _bmm('hid,hie->hde', kd[ps], vn)
        on = _head_rms(o)
        for h in range(N_HEADS):
            hs = _head_slice(h)
            out_ref[rows(n), hs] = on[h] * ng_ref[:, hs] * _silu(z[rows(n), hs])
    s_scr[...] = s

    @pl.when(t == last)
    def _():
        ns_ref[0] = s


def _gdn_sample_kernel(x_ref, cw_ref, alog_ref, dtb_ref, ng_ref, buf_ref, s0_ref,
                       out_ref, nconv_ref, ns_ref, ext, y_scr, *, nseq, c):
    w = GDN_WIDTH
    hist = SUBLANES - (CONV_W - 1)
    r = nseq * c

    def conv_body(u, carry_):
        r0 = pl.multiple_of(u * c, SUBLANES)
        ext[hist:SUBLANES, :] = buf_ref[u]
        y_scr[pl.ds(r0, c), :] = _conv_unit(ext, x_ref[pl.ds(r0, c), 0:3 * w], cw_ref, c)
        nconv_ref[u] = ext[c + hist:c + SUBLANES, :]
        return carry_

    lax.fori_loop(0, nseq, conv_body, 0)
    qkv = _silu(y_scr[...])
    lower, strict, upper = _segment_masks(r, c)
    g_col, g_row, beta = _gdn_gates(x_ref[:, 4 * w:4 * w + LANES], alog_ref, dtb_ref, lower, upper)
    g_seq = g_col.reshape(nseq, c, LANES)
    g_end = g_seq[:, c - 1:c, :]
    g_end_rows = jnp.broadcast_to(g_end, (nseq, c, LANES)).reshape(r, LANES)
    qn = [a * (HEAD_DIM ** -0.5) for a in _l2norm_heads(qkv, 0)]
    kn = _l2norm_heads(qkv, w)
    heads = range(N_HEADS)
    hstack = lambda fn: jnp.stack([fn(h) for h in heads], axis=0)
    q = hstack(lambda h: qn[h])
    k = hstack(lambda h: kn[h])
    v = hstack(lambda h: qkv[:, _head_slice(h, 2 * w)])
    gi = hstack(lambda h: g_col[:, h:h + 1])
    gj = hstack(lambda h: g_row[h:h + 1, :])
    bi = hstack(lambda h: beta[:, GDN_BETA_LANE + h:GDN_BETA_LANE + h + 1])
    ge = hstack(lambda h: g_end_rows[:, h:h + 1])
    eye = (lower & jnp.logical_not(strict)).astype(F32)[None]
    decay = jnp.exp(jnp.where(lower[None], gi - gj, -jnp.inf))
    kb = k.astype(BF16)
    a_mat = jnp.where(strict[None], bi * _bmm('bid,bjd->bij', kb, kb) * decay, 0.0)
    tm = _unit_lower_inverse(a_mat, eye, c)
    eg = jnp.exp(gi)
    rhs = jnp.concatenate([bi * v, (bi * eg) * k], axis=2)
    uw = _bmm('bij,bjk->bik', tm.astype(BF16), rhs.astype(BF16))
    pm = (_bmm('bid,bjd->bij', q.astype(BF16), kb) * decay).astype(BF16)
    qg = q * eg
    kd = k * jnp.exp(ge - gi)
    per_seq = lambda a: a.reshape(nseq, c, HEAD_DIM)
    s0 = [s0_ref[:, h] for h in heads]
    wqs = [_bmm('bid,bde->bie',
                jnp.concatenate([per_seq(uw[h][:, HEAD_DIM:2 * HEAD_DIM]), per_seq(qg[h])], axis=1), s0[h])
           for h in heads]
    vn = [per_seq(uw[h][:, 0:HEAD_DIM]) - wqs[h][:, 0:c] for h in heads]
    o = [wqs[h][:, c:2 * c].reshape(r, HEAD_DIM)
         + _dot(pm[h], vn[h].reshape(r, HEAD_DIM).astype(BF16)) for h in heads]
    upd = [_bmm('bid,bie->bde', per_seq(kd[h]), vn[h]) for h in heads]
    z = x_ref[:, 3 * w:4 * w]
    for h in heads:
        hs = _head_slice(h)
        ns_ref[:, h] = jnp.exp(g_end[:, :, h:h + 1]) * s0[h] + upd[h]
        out_ref[:, hs] = _head_rms(o[h]) * ng_ref[:, hs] * _silu(z[:, hs])


def _gdn(x, cw, alog, dtb, ng, state, *, batch, seq):
    w = GDN_WIDTH
    n = batch * seq
    c = math.gcd(seq, CHUNK)
    if state is None:
        r = PROMPT_BLOCK
        nt = seq // r
        rows = lambda b, t: (b * nt + t, 0)
        const = lambda b, t: (0, 0)
        st3 = lambda b, t: (b, 0, 0)
        st4 = lambda b, t: (b, 0, 0, 0)
        return pl.pallas_call(
            functools.partial(_gdn_prompt_kernel, r=r, c=c),
            grid=(batch, nt),
            in_specs=[pl.BlockSpec((r, GDN_COLS), rows),
                      pl.BlockSpec((CONV_W, 3 * w), const),
                      pl.BlockSpec((1, LANES), const),
                      pl.BlockSpec((1, LANES), const),
                      pl.BlockSpec((1, w), const)],
            out_specs=[pl.BlockSpec((r, w), rows),
                       pl.BlockSpec((1, CONV_W - 1, 3 * w), st3),
                       pl.BlockSpec((1, N_HEADS, HEAD_DIM, HEAD_DIM), st4)],
            out_shape=[jax.ShapeDtypeStruct((n, w), F32),
                       jax.ShapeDtypeStruct((batch, CONV_W - 1, 3 * w), F32),
                       jax.ShapeDtypeStruct((batch, N_HEADS, HEAD_DIM, HEAD_DIM), F32)],
            scratch_shapes=[pltpu.VMEM((r + SUBLANES, 3 * w), F32),
                            pltpu.VMEM((N_HEADS, HEAD_DIM, HEAD_DIM), F32)],
            compiler_params=_cparams(("parallel", "arbitrary")),
            name="gdn_prompt",
        )(x, cw, alog, dtb, ng)
    nseq = SAMPLE_SEQS
    r = nseq * c
    rows = lambda i: (i, 0)
    const = lambda i: (0, 0)
    st3 = lambda i: (i, 0, 0)
    st4 = lambda i: (i, 0, 0, 0)
    cblk = (nseq, CONV_W - 1, 3 * w)
    sblk = (nseq, N_HEADS, HEAD_DIM, HEAD_DIM)
    return pl.pallas_call(
        functools.partial(_gdn_sample_kernel, nseq=nseq, c=c),
        grid=(batch // nseq,),
        in_specs=[pl.BlockSpec((r, GDN_COLS), rows),
                  pl.BlockSpec((CONV_W, 3 * w), const),
                  pl.BlockSpec((1, LANES), const),
                  pl.BlockSpec((1, LANES), const),
                  pl.BlockSpec((1, w), const),
                  pl.BlockSpec(cblk, st3),
                  pl.BlockSpec(sblk, st4)],
        out_specs=[pl.BlockSpec((r, w), rows), pl.BlockSpec(cblk, st3), pl.BlockSpec(sblk, st4)],
        out_shape=[jax.ShapeDtypeStruct((n, w), F32),
                   jax.ShapeDtypeStruct((batch, CONV_W - 1, 3 * w), F32),
                   jax.ShapeDtypeStruct((batch, N_HEADS, HEAD_DIM, HEAD_DIM), F32)],
        scratch_shapes=[pltpu.VMEM((c + SUBLANES, 3 * w), F32), pltpu.VMEM((r, 3 * w), F32)],
        compiler_params=_cparams(("parallel",)),
        name="gdn_sample",
    )(x, cw, alog, dtb, ng, *state)


def _block_diag(wb):
    out = jnp.zeros((LRU_WIDTH, LRU_WIDTH), wb.dtype)
    for n in range(LRU_BLOCKS):
        s = slice(n * LRU_BLOCK, (n + 1) * LRU_BLOCK)
        out = out.at[s, s].set(wb[n])
    return out


def _pad_lanes(v, offset=0):
    return jnp.zeros((1, LANES), F32).at[0, offset:offset + v.shape[0]].set(v.astype(F32))


def _prep_layer(W, l):
    row = lambda v: v.reshape(1, -1).astype(F32)
    pad = IN_COLS - W['w_in'].shape[2]
    wr = jnp.zeros((D_MODEL, LANES), F32)
    wr = wr.at[:, ROUTER_GROUP_LANE:ROUTER_GROUP_LANE + N_GROUPS].set(W['w_router_group'][l])
    wr = wr.at[:, ROUTER_EXPERT_LANE:ROUTER_EXPERT_LANE + N_EXPERTS].set(W['w_router_expert'][l])
    br = jnp.zeros((1, LANES), F32)
    br = br.at[0, ROUTER_GROUP_LANE:ROUTER_GROUP_LANE + N_GROUPS].set(W['b_router_group'][l])
    br = br.at[0, ROUTER_EXPERT_LANE:ROUTER_EXPERT_LANE + N_EXPERTS].set(W['b_router_expert'][l])
    return dict(
        norm_mix_g=row(W['norm_mix_g'][l]),
        w_in=jnp.pad(W['w_in'][l], ((0, 0), (0, pad))).astype(BF16),
        lru_conv_w=W['lru_conv_w'][l], lru_conv_b=row(W['lru_conv_b'][l]),
        lru_wa=_block_diag(W['lru_wa'][l]).astype(BF16), lru_ba=row(W['lru_ba'][l]),
        lru_wx=_block_diag(W['lru_wx'][l]).astype(BF16), lru_bx=row(W['lru_bx'][l]),
        lru_lambda=row(W['lru_lambda'][l]),
        ret_norm_g=row(W['ret_norm_g'][l]),
        gdn_conv_w=W['gdn_conv_w'][l],
        gdn_a_log=_pad_lanes(W['gdn_a_log'][l]), gdn_dt_bias=_pad_lanes(W['gdn_dt_bias'][l]),
        gdn_norm_g=jnp.tile(row(W['gdn_norm_g'][l]), (1, N_HEADS)),
        w_out=W['w_out'][l].astype(BF16),
        norm_ffn_g=row(W['norm_ffn_g'][l]),
        w_router=wr.astype(BF16), b_router=br,
        w_expert_gate=W['w_expert_gate'][l].astype(BF16),
        w_expert_up=W['w_expert_up'][l].astype(BF16),
        w_expert_down=W['w_expert_down'][l].astype(BF16),
        norm_pe_g=row(W['norm_pe_g'][l]),
        w_pe=W['w_pe'][l].astype(BF16),
        w_pe_gate=W['w_pe_gate'][l].astype(BF16),
    )


def _trunk(x, p, states, pos0, layers, final_g):
    batch, seq, _ = x.shape
    n = batch * seq
    h = x.reshape(n, D_MODEL)
    new = ([], [], [], [], [])
    for l, L in enumerate(layers):
        if states is None:
            st_lru = st_ret = st_gdn = None
        else:
            st_lru = (states[0][l], states[1][l])
            st_ret = states[2][l]
            st_gdn = (states[3][l], states[4][l])
        lru_x, ret_x, gdn_x = _in_proj(h, L['norm_mix_g'], L['w_in'])
        out_a, n_lru_conv, n_lru_h = _lru(
            lru_x, L['lru_conv_w'], L['lru_conv_b'], L['lru_wa'], L['lru_ba'], L['lru_wx'],
            L['lru_bx'], L['lru_lambda'], st_lru, batch=batch, seq=seq)
        out_b, n_ret = _ret(ret_x, L['ret_norm_g'], st_ret, batch=batch, seq=seq, pos0=pos0)
        out_c, n_gdn_conv, n_gdn = _gdn(
            gdn_x, L['gdn_conv_w'], L['gdn_a_log'], L['gdn_dt_bias'], L['gdn_norm_g'], st_gdn,
            batch=batch, seq=seq)
        h2 = _ffn(h, out_a, out_b, out_c, L['w_out'], L['norm_ffn_g'], L['w_router'], L['b_router'],
                  L['w_expert_gate'], L['w_expert_up'], L['w_expert_down'])
        h = _pe(h2, p[l].reshape(n, PE_DIM), L['norm_pe_g'], L['w_pe_gate'], L['w_pe'], final_g,
                final=(l == len(layers) - 1))
        for lst, s in zip(new, (n_lru_conv, n_lru_h, n_ret, n_gdn_conv, n_gdn)):
            lst.append(s)
    return h.reshape(batch, seq, D_MODEL), tuple(jnp.stack(lst) for lst in new)


def kernel(x_prompt, x_sample, p_prompt, p_sample, state_lru_conv, state_lru_h, state_ret, state_gdn_conv, state_gdn, norm_mix_g, w_in, lru_conv_w, lru_conv_b, lru_wa, lru_ba, lru_wx, lru_bx, lru_lambda, ret_norm_g, gdn_conv_w, gdn_a_log, gdn_dt_bias, gdn_norm_g, w_out, norm_ffn_g, w_router_group, b_router_group, w_router_expert, b_router_expert, w_expert_gate, w_expert_up, w_expert_down, norm_pe_g, w_pe, w_pe_gate, final_norm_g):
    W = dict(norm_mix_g=norm_mix_g, w_in=w_in, lru_conv_w=lru_conv_w, lru_conv_b=lru_conv_b,
             lru_wa=lru_wa, lru_ba=lru_ba, lru_wx=lru_wx, lru_bx=lru_bx, lru_lambda=lru_lambda,
             ret_norm_g=ret_norm_g, gdn_conv_w=gdn_conv_w, gdn_a_log=gdn_a_log, gdn_dt_bias=gdn_dt_bias,
             gdn_norm_g=gdn_norm_g, w_out=w_out, norm_ffn_g=norm_ffn_g, w_router_group=w_router_group,
             b_router_group=b_router_group, w_router_expert=w_router_expert, b_router_expert=b_router_expert,
             w_expert_gate=w_expert_gate, w_expert_up=w_expert_up, w_expert_down=w_expert_down,
             norm_pe_g=norm_pe_g, w_pe=w_pe, w_pe_gate=w_pe_gate)
    layers = [_prep_layer(W, l) for l in range(DEPTH)]
    final_g = final_norm_g.reshape(1, D_MODEL).astype(F32)
    y_p, st_p = _trunk(x_prompt, p_prompt, None, 0, layers, final_g)
    sample_states = (state_lru_conv, state_lru_h, state_ret, state_gdn_conv, state_gdn)
    y_s, st_s = _trunk(x_sample, p_sample, sample_states, PAST_LEN, layers, final_g)
    return (y_p, y_s) + st_p + st_s
```
